```python
import math
import jax, jax.numpy as jnp
from jax import lax
import numpy as np

D_MODEL = 1024
BATCH = 4
SEQ = 8192
DEPTH = 2

CTX_LEN = 256
GRID_W = 64
N_MOD = 9
FFN_HIDDEN = 2816
EPS = 1e-6
MIX_W = 512
N_BRANCH = 4
CHUNK = 64
HG_HEADS = 4
HG_DK = 128
HG_DV = 128
S5_GROUP = 16
S5_GROUPS = MIX_W // S5_GROUP
S5_STATE = 64
HY_ORDER = 2
HY_BANDS = 16
HY_EMB = 1 + 2 * HY_BANDS
HY_FFN = 64
HY_INNER = 2
HY_MIN_DECAY = math.log(1e-2) / 1.5
HY_MAX_DECAY = math.log(1e-2) / 0.3
GLA_HEADS = 4
GLA_DK = 64
GLA_DV = 128
GLA_RANK = 16
GLA_GATE_NORM = 16.0
HG_KEY = HG_HEADS * HG_DK
GLA_KEY = GLA_HEADS * GLA_DK
IN_SPLIT = (HG_KEY, HG_KEY, HG_KEY, MIX_W, MIX_W, MIX_W, (HY_ORDER + 1) * MIX_W,
            GLA_KEY, GLA_KEY, MIX_W, MIX_W, 2 * GLA_RANK, N_BRANCH * D_MODEL)
IN_COLS = sum(IN_SPLIT)

kernel_name = 'hybrid_gated_mixer_dit_block'


def rmsnorm(x, g):
    xf = x.astype(jnp.float32)
    y = xf * lax.rsqrt(jnp.mean(xf * xf, axis=-1, keepdims=True) + EPS)
    return (y * g.astype(jnp.float32)).astype(x.dtype)


def modulated_norm(x, g, shift, scale):
    return rmsnorm(x, g) * (1.0 + scale) + shift


def add_residual(x, y, g_post, gate, weight):
    return x + weight * gate * rmsnorm(y, g_post)


def swiglu(h, w_in, w_out):
    a, b = jnp.split(h @ w_in, 2, axis=-1)
    return (jax.nn.silu(a) * b) @ w_out


def ffn_sublayer(x, mods, g_pre, g_post, w_in, w_out):
    shift, scale, gate = mods
    y = swiglu(modulated_norm(x, g_pre, shift, scale), w_in, w_out)
    return add_residual(x, y, g_post, gate, 0.5)


def heads(a, n, d):
    b, l, _ = a.shape
    return a.astype(jnp.float32).reshape(b, l, n, d)


def _to_chunks(a):
    b, l, h, d = a.shape
    return a.reshape(b, l // CHUNK, CHUNK, h, d).transpose(1, 0, 3, 2, 4)


def _from_chunks(a):
    n, b, h, c, d = a.shape
    return a.transpose(1, 0, 3, 2, 4).reshape(b, n * c, h, d)


def chunked_gated_recurrence(q, k, v, log_f, s0):
    mask = jnp.tril(jnp.ones((CHUNK, CHUNK), dtype=bool))[:, :, None]

    def step(s, inp):
        qc, kc, vc, gc = inp
        g = jnp.cumsum(gc, axis=2)
        diff = g[:, :, :, None, :] - g[:, :, None, :, :]
        decay = jnp.exp(jnp.where(mask, diff, -jnp.inf))
        scores = jnp.einsum('bhtc,bhsc,bhtsc->bhts', qc, kc, decay)
        o = (jnp.einsum('bhts,bhsv->bhtv', scores, vc)
             + jnp.einsum('bhtc,bhcv->bhtv', qc * jnp.exp(g), s))
        g_last = g[:, :, -1:, :]
        s = (jnp.exp(g_last[:, :, 0, :, None]) * s
             + jnp.einsum('bhsc,bhsv->bhcv', kc * jnp.exp(g_last - g), vc))
        return s, o

    s_final, o = lax.scan(step, s0, (_to_chunks(q), _to_chunks(k), _to_chunks(v), _to_chunks(log_f)))
    return _from_chunks(o), s_final


def bidir_recurrence(q, k_f, k_b, v, lf_f, lf_b, s0_f, s0_b):
    o_f, s_f = chunked_gated_recurrence(q, k_f, v, lf_f, s0_f)
    flip = lambda a: jnp.flip(a, axis=1)
    o_b, s_b = chunked_gated_recurrence(flip(q), flip(k_b), flip(v), flip(lf_b), s0_b)
    return o_f + flip(o_b), s_f, s_b


def gated_head_norm(o, gnorm, g):
    b, l = o.shape[:2]
    y = rmsnorm(o, gnorm).reshape(b, l, -1) * jax.nn.silu(g.astype(jnp.float32))
    return y.astype(g.dtype)


def lower_bound(logits, layer):
    cum = jnp.cumsum(jax.nn.softmax(logits.astype(jnp.float32), axis=0), axis=0)
    return cum[layer] - cum[0]


def hgrn2_scan(hq, hf_f, hf_b, hi, lb_f, lb_b, s0_f, s0_b):
    q = heads(jax.nn.silu(hq), HG_HEADS, HG_DK) * HG_DK ** -0.5
    v = heads(hi, HG_HEADS, HG_DV)

    def gate(hf, lb):
        z = heads(hf, HG_HEADS, HG_DK)
        lb = lb.reshape(HG_HEADS, HG_DK)
        log_f = jnp.logaddexp(jnp.log(lb), jnp.log1p(-lb) + jax.nn.log_sigmoid(z))
        k = (1.0 - lb) * jax.nn.sigmoid(-z)
        return k, log_f

    k_f, lf_f = gate(hf_f, lb_f)
    k_b, lf_b = gate(hf_b, lb_b)
    return bidir_recurrence(q, k_f, k_b, v, lf_f, lf_b, s0_f, s0_b)


def gla_scan(gq, gk, gv, glr, gate_w, gate_b, s0_f, s0_b):
    q = heads(gq, GLA_HEADS, GLA_DK) * GLA_DK ** -0.5
    k = heads(gk, GLA_HEADS, GLA_DK)
    v = heads(gv, GLA_HEADS, GLA_DV)
    lr_f, lr_b = jnp.split(glr, 2, axis=-1)

    def gate(lr, d):
        z = heads(lr @ gate_w[d] + gate_b[d], GLA_HEADS, GLA_DK)
        return jax.nn.log_sigmoid(z) / GLA_GATE_NORM

    return bidir_recurrence(q, k, k, v, gate(lr_f, 0), gate(lr_b, 1), s0_f, s0_b)


def _linear_combine(e1, e2):
    a1, b1 = e1
    a2, b2 = e2
    return a1 * a2, a2 * b1 + b2


def s5_scan(u, a_re, a_im, log_dt, b_re, b_im, x0_f, x0_b):
    b, l, _ = u.shape
    uc = u.astype(jnp.float32).reshape(b, l, S5_GROUPS, S5_GROUP).astype(jnp.complex64)
    bmat = lax.complex(b_re.astype(jnp.float32), b_im.astype(jnp.float32))

    def direction(d, x0, reverse):
        lam = lax.complex(a_re[d].astype(jnp.float32), a_im[d].astype(jnp.float32))
        dt = jnp.exp(log_dt[d].astype(jnp.float32))[:, None]
        lam_bar = jnp.exp(lam * dt)
        b_bar = ((lam_bar - 1.0) / lam)[:, :, None] * bmat
        bu = jnp.einsum('blgh,gph->blgp', uc, b_bar)
        edge = -1 if reverse else 0
        bu = bu.at[:, edge].add(lam_bar * x0)
        a = jnp.broadcast_to(lam_bar, (1, l) + lam_bar.shape)
        _, xs = lax.associative_scan(_linear_combine, (a, bu), reverse=reverse, axis=1)
        return xs, xs[:, 0 if reverse else -1]

    xs_f, xf = direction(0, x0_f, False)
    xs_b, xb = direction(1, x0_b, True)
    return xs_f + xs_b, xf, xb


def s5_readout(xs, u, c_re, c_im, d, glu_w, glu_b):
    b, l, _ = u.shape
    cmat = lax.complex(c_re.astype(jnp.float32), c_im.astype(jnp.float32))
    y = jnp.einsum('blgp,ghp->blgh', xs, cmat).real.reshape(b, l, MIX_W)
    y = y + d.astype(jnp.float32) * u.astype(jnp.float32)
    y = jax.nn.gelu(y).astype(u.dtype)
    a, g = jnp.split(y @ glu_w + glu_b, 2, axis=-1)
    return a * jax.nn.sigmoid(g)


def depthwise_conv3(u, w, bias):
    y = lax.conv_general_dilated(u, w[:, None, :].astype(u.dtype), window_strides=(1,),
                                 padding=((1, 1),), dimension_numbers=('NWC', 'WIO', 'NWC'),
                                 feature_group_count=u.shape[-1])
    return y + bias


def hyena_filters(l, fw_in, fb_in, fw_mid, fb_mid, fw_out, freq):
    f32 = jnp.float32
    t = jnp.linspace(0.0, 1.0, l, dtype=f32)[:, None]
    pos = jnp.arange(l, dtype=f32)[:, None]
    bands = jnp.linspace(1e-4, HY_BANDS - 1, HY_BANDS, dtype=f32)
    ang = (2.0 * math.pi / l) * pos * bands
    feat = jnp.concatenate([t, jnp.cos(ang), -jnp.sin(ang)], axis=-1)
    w = freq.astype(f32)
    h = jnp.sin(w * (feat @ fw_in.astype(f32) + fb_in.astype(f32)))
    for j in range(HY_INNER):
        h = jnp.sin(w * (h @ fw_mid[j].astype(f32) + fb_mid[j].astype(f32)))
    h = h @ fw_out.astype(f32)
    deltas = jnp.abs(jnp.linspace(HY_MIN_DECAY, HY_MAX_DECAY, HY_ORDER * MIX_W, dtype=f32))
    h = h * jnp.exp(-t * deltas)
    h = h * lax.rsqrt(2.0 * jnp.sum(h * h, axis=0, keepdims=True) - h[:1] ** 2 + EPS)
    return h.reshape(l, HY_ORDER, MIX_W)


def long_conv(z, h, bias):
    l = z.shape[1]
    k = jnp.concatenate([h, jnp.zeros_like(h[:1]), jnp.flip(h[1:], axis=0)], axis=0)
    zf = jnp.fft.rfft(z, n=2 * l, axis=1)
    kf = jnp.fft.rfft(k, axis=0)
    y = jnp.fft.irfft(zf * kf[None], n=2 * l, axis=1)[:, :l]
    return y + bias * z


def hyena_mixer(u, rows, p):
    b, l, c3 = u.shape
    if rows is None:
        uc = depthwise_conv3(u, p['hy_conv_w'], p['hy_conv_b'])
    else:
        uc = depthwise_conv3(u.reshape(b * rows, GRID_W, c3), p['hy_conv_w'],
                             p['hy_conv_b']).reshape(b, l, c3)
    v, x1, x2 = jnp.split(uc, HY_ORDER + 1, axis=-1)
    filt = hyena_filters(l, p['hy_fw_in'], p['hy_fb_in'], p['hy_fw_mid'], p['hy_fb_mid'],
                         p['hy_fw_out'], p['hy_freq'])
    z = v.astype(jnp.float32)
    for n, gate in enumerate((x1, x2)):
        z = gate.astype(jnp.float32) * long_conv(z, filt[:, n], p['hy_bias'][n].astype(jnp.float32))
    return z.astype(u.dtype)


def zero_states(b):
    return (jnp.zeros((b, HG_HEADS, HG_DK, HG_DV), jnp.float32),
            jnp.zeros((b, HG_HEADS, HG_DK, HG_DV), jnp.float32),
            jnp.zeros((b, S5_GROUPS, S5_STATE), jnp.complex64),
            jnp.zeros((b, S5_GROUPS, S5_STATE), jnp.complex64),
            jnp.zeros((b, GLA_HEADS, GLA_DK, GLA_DV), jnp.float32),
            jnp.zeros((b, GLA_HEADS, GLA_DK, GLA_DV), jnp.float32))


def token_mixer(h, p, states, rows, with_output):
    b, l, _ = h.shape
    idx = [int(i) for i in np.cumsum(IN_SPLIT)[:-1]]
    (hq, hf_f, hf_b, hi, hg, s5_u, hy_u, gq, gk, gv, gg, glr, gates) = jnp.split(
        h @ p['w_in'], idx, axis=-1)
    s_hg_f, s_hg_b, x_s5_f, x_s5_b, s_gla_f, s_gla_b = states
    o_hg, s_hg_f, s_hg_b = hgrn2_scan(hq, hf_f, hf_b, hi, p['hg_lb_fwd'], p['hg_lb_bwd'],
                                      s_hg_f, s_hg_b)
    xs_s5, x_s5_f, x_s5_b = s5_scan(s5_u, p['s5_a_re'], p['s5_a_im'], p['s5_log_dt'],
                                    p['s5_b_re'], p['s5_b_im'], x_s5_f, x_s5_b)
    o_gla, s_gla_f, s_gla_b = gla_scan(gq, gk, gv, glr, p['gla_gate_w'], p['gla_gate_b'],
                                       s_gla_f, s_gla_b)
    new_states = (s_hg_f, s_hg_b, x_s5_f, x_s5_b, s_gla_f, s_gla_b)
    if not with_output:
        return None, new_states
    branches = (gated_head_norm(o_hg, p['hg_gnorm'], hg),
                s5_readout(xs_s5, s5_u, p['s5_c_re'], p['s5_c_im'], p['s5_d'],
                           p['s5_glu_w'], p['s5_glu_b']),
                hyena_mixer(hy_u, rows, p),
                gated_head_norm(o_gla, p['gla_gnorm'], gg))
    gates = jax.nn.sigmoid(gates.reshape(b, l, N_BRANCH, D_MODEL))
    merged = gates[:, :, 0] * (branches[0] @ p['w_branch'][0])
    for n in range(1, N_BRANCH):
        merged = merged + gates[:, :, n] * (branches[n] @ p['w_branch'][n])
    return merged @ p['w_out'], new_states


def setup_inputs(seed: int = 0) -> dict:
    key = jax.random.key(seed)
    ks = iter(jax.random.split(key, 48))

    def nrm(shape, scale):
        return scale * jax.random.normal(next(ks), shape, jnp.float32)

    def gain(shape):
        return 1.0 + nrm(shape, 0.02)

    d, f, w = D_MODEL, FFN_HIDDEN, MIX_W
    n_idx = jnp.arange(S5_STATE, dtype=jnp.float32)
    return dict(
        x=nrm((BATCH, SEQ, d), 1.0),
        c=nrm((BATCH, d), 1.0),
        ctx=nrm((BATCH, CTX_LEN, d), 1.0),
        c_ctx=nrm((d,), 1.0),
        ada_w=nrm((DEPTH, d, N_MOD * d), d ** -0.5),
        ada_b=nrm((DEPTH, N_MOD * d), 0.02),
        norm_pre=gain((DEPTH, 3, d)),
        norm_post=gain((DEPTH, 3, d)),
        ffn1_w_in=nrm((DEPTH, d, 2 * f), d ** -0.5),
        ffn1_w_out=nrm((DEPTH, f, d), f ** -0.5),
        ffn2_w_in=nrm((DEPTH, d, 2 * f), d ** -0.5),
        ffn2_w_out=nrm((DEPTH, f, d), f ** -0.5),
        w_in=nrm((DEPTH, d, IN_COLS), d ** -0.5),
        hg_lb_fwd=nrm((DEPTH, HG_KEY), 0.5),
        hg_lb_bwd=nrm((DEPTH, HG_KEY), 0.5),
        hg_gnorm=gain((DEPTH, HG_DV)),
        s5_a_re=-0.5 + nrm((DEPTH, 2, S5_GROUPS, S5_STATE), 0.01),
        s5_a_im=jnp.broadcast_to(math.pi * n_idx, (DEPTH, 2, S5_GROUPS, S5_STATE)),
        s5_log_dt=jax.random.uniform(next(ks), (DEPTH, 2, S5_GROUPS), jnp.float32,
                                     math.log(1e-3), math.log(1e-1)),
        s5_b_re=nrm((DEPTH, S5_GROUPS, S5_STATE, S5_GROUP), (2 * S5_GROUP) ** -0.5),
        s5_b_im=nrm((DEPTH, S5_GROUPS, S5_STATE, S5_GROUP), (2 * S5_GROUP) ** -0.5),
        s5_c_re=nrm((DEPTH, S5_GROUPS, S5_GROUP, S5_STATE), (2 * S5_STATE) ** -0.5),
        s5_c_im=nrm((DEPTH, S5_GROUPS, S5_GROUP, S5_STATE), (2 * S5_STATE) ** -0.5),
        s5_d=nrm((DEPTH, w), 1.0),
        s5_glu_w=nrm((DEPTH, w, 2 * w), w ** -0.5),
        s5_glu_b=nrm((DEPTH, 2 * w), 0.02),
        hy_conv_w=nrm((DEPTH, 3, (HY_ORDER + 1) * w), 3 ** -0.5),
        hy_conv_b=nrm((DEPTH, (HY_ORDER + 1) * w), 0.02),
        hy_fw_in=nrm((DEPTH, HY_EMB, HY_FFN), HY_EMB ** -0.5),
        hy_fb_in=nrm((DEPTH, HY_FFN), 0.02),
        hy_fw_mid=nrm((DEPTH, HY_INNER, HY_FFN, HY_FFN), HY_FFN ** -0.5),
        hy_fb_mid=nrm((DEPTH, HY_INNER, HY_FFN), 0.02),
        hy_fw_out=nrm((DEPTH, HY_FFN, HY_ORDER * w), HY_FFN ** -0.5),
        hy_freq=gain((DEPTH, HY_FFN)),
        hy_bias=nrm((DEPTH, HY_ORDER, w), 1.0),
        gla_gate_w=nrm((DEPTH, 2, GLA_RANK, GLA_KEY), GLA_RANK ** -0.5),
        gla_gate_b=nrm((DEPTH, 2, GLA_KEY), 0.02),
        gla_gnorm=gain((DEPTH, GLA_DV)),
        w_branch=nrm((DEPTH, N_BRANCH, w, d), w ** -0.5),
        w_out=nrm((DEPTH, d, d), d ** -0.5),
    )


def reference(x, c, ctx, c_ctx, ada_w, ada_b, norm_pre, norm_post, ffn1_w_in, ffn1_w_out,
              ffn2_w_in, ffn2_w_out, w_in, hg_lb_fwd, hg_lb_bwd, hg_gnorm, s5_a_re, s5_a_im,
              s5_log_dt, s5_b_re, s5_b_im, s5_c_re, s5_c_im, s5_d, s5_glu_w, s5_glu_b,
              hy_conv_w, hy_conv_b, hy_fw_in, hy_fb_in, hy_fw_mid, hy_fb_mid, hy_fw_out,
              hy_freq, hy_bias, gla_gate_w, gla_gate_b, gla_gnorm, w_branch, w_out):
    b = x.shape[0]
    rows = x.shape[1] // GRID_W
    c_act = jax.nn.silu(c)
    cc_act = jax.nn.silu(c_ctx)
    xc = ctx
    for l in range(DEPTH):
        last = l == DEPTH - 1
        p = dict(w_in=w_in[l], hg_lb_fwd=lower_bound(hg_lb_fwd, l),
                 hg_lb_bwd=lower_bound(hg_lb_bwd, l), hg_gnorm=hg_gnorm[l],
                 s5_a_re=s5_a_re[l], s5_a_im=s5_a_im[l], s5_log_dt=s5_log_dt[l],
                 s5_b_re=s5_b_re[l], s5_b_im=s5_b_im[l], s5_c_re=s5_c_re[l], s5_c_im=s5_c_im[l],
                 s5_d=s5_d[l], s5_glu_w=s5_glu_w[l], s5_glu_b=s5_glu_b[l],
                 hy_conv_w=hy_conv_w[l], hy_conv_b=hy_conv_b[l], hy_fw_in=hy_fw_in[l],
                 hy_fb_in=hy_fb_in[l], hy_fw_mid=hy_fw_mid[l], hy_fb_mid=hy_fb_mid[l],
                 hy_fw_out=hy_fw_out[l], hy_freq=hy_freq[l], hy_bias=hy_bias[l],
                 gla_gate_w=gla_gate_w[l], gla_gate_b=gla_gate_b[l], gla_gnorm=gla_gnorm[l],
                 w_branch=w_branch[l], w_out=w_out[l])
        mod = jnp.split((c_act @ ada_w[l] + ada_b[l])[:, None, :], N_MOD, axis=-1)
        modc = jnp.split(cc_act @ ada_w[l] + ada_b[l], N_MOD, axis=-1)
        x = ffn_sublayer(x, mod[0:3], norm_pre[l, 0], norm_post[l, 0], ffn1_w_in[l], ffn1_w_out[l])
        xc = ffn_sublayer(xc, modc[0:3], norm_pre[l, 0], norm_post[l, 0], ffn1_w_in[l], ffn1_w_out[l])
        hc = modulated_norm(xc, norm_pre[l, 1], modc[3], modc[4])
        hx = modulated_norm(x, norm_pre[l, 1], mod[3], mod[4])
        yc, ctx_states = token_mixer(hc, p, zero_states(b), None, not last)
        yx, _ = token_mixer(hx, p, ctx_states, rows, True)
        x = add_residual(x, yx, norm_post[l, 1], mod[5], 1.0)
        x = ffn_sublayer(x, mod[6:9], norm_pre[l, 2], norm_post[l, 2], ffn2_w_in[l], ffn2_w_out[l])
        if not last:
            xc = add_residual(xc, yc, norm_post[l, 1], modc[5], 1.0)
            xc = ffn_sublayer(xc, modc[6:9], norm_pre[l, 2], norm_post[l, 2],
                              ffn2_w_in[l], ffn2_w_out[l])
    return x
```

```python
import functools
import math

import jax
import jax.numpy as jnp
import numpy as np
from jax import lax
from jax.experimental import pallas as pl
from jax.experimental.pallas import tpu as pltpu

D_MODEL = 1024
GRID_W = 64
N_MOD = 9
FFN_HIDDEN = 2816
EPS = 1e-6
MIX_W = 512
N_BRANCH = 4
CHUNK = 64
HG_HEADS = 4
HG_DK = 128
HG_DV = 128
S5_GROUP = 16
S5_GROUPS = MIX_W // S5_GROUP
S5_STATE = 64
HY_ORDER = 2
HY_BANDS = 16
HY_INNER = 2
HY_MIN_DECAY = math.log(1e-2) / 1.5
HY_MAX_DECAY = math.log(1e-2) / 0.3
GLA_HEADS = 4
GLA_DK = 64
GLA_DV = 128
GLA_RANK = 16
GLA_GATE_NORM = 16.0
HG_KEY = HG_HEADS * HG_DK
GLA_KEY = GLA_HEADS * GLA_DK
IN_SPLIT = (HG_KEY, HG_KEY, HG_KEY, MIX_W, MIX_W, MIX_W, (HY_ORDER + 1) * MIX_W,
            GLA_KEY, GLA_KEY, MIX_W, MIX_W, 2 * GLA_RANK, N_BRANCH * D_MODEL)

VMEM_LIMIT_BYTES = 56 * 1024 * 1024
FFN_TOKEN_TILE = 512
FFN_HIDDEN_CHUNK = 256


def _ffn_body(x_ref, mod_ref, gpre_ref, gpost_ref, win_ref, wout_ref, o_ref, acc_ref, *, hidden, chunk):
    x = x_ref[0]
    xn = x * lax.rsqrt(jnp.mean(x * x, axis=-1, keepdims=True) + EPS) * gpre_ref[...]
    shift = mod_ref[0, 0:1, :]
    scale = mod_ref[0, 1:2, :]
    gate = mod_ref[0, 2:3, :]
    h = (xn * (1.0 + scale) + shift).astype(jnp.bfloat16)
    for j in range(hidden // chunk):
        a = jnp.dot(h, win_ref[:, j * chunk:(j + 1) * chunk], preferred_element_type=jnp.float32)
        b = jnp.dot(h, win_ref[:, hidden + j * chunk:hidden + (j + 1) * chunk],
                    preferred_element_type=jnp.float32)
        g = (a * jax.nn.sigmoid(a) * b).astype(jnp.bfloat16)
        part = jnp.dot(g, wout_ref[j * chunk:(j + 1) * chunk, :], preferred_element_type=jnp.float32)
        if j == 0:
            acc_ref[...] = part
        else:
            acc_ref[...] += part
    y = acc_ref[...]
    yn = y * lax.rsqrt(jnp.mean(y * y, axis=-1, keepdims=True) + EPS) * gpost_ref[...]
    o_ref[0] = x + 0.5 * gate * yn


def ffn_sublayer(x, mods, g_pre, g_post, w_in, w_out):
    b, l, d = x.shape
    hidden = w_out.shape[0]
    tm = min(FFN_TOKEN_TILE, l)
    assert l % tm == 0 and hidden % FFN_HIDDEN_CHUNK == 0
    const = lambda i, j: (0, 0)
    return pl.pallas_call(
        functools.partial(_ffn_body, hidden=hidden, chunk=FFN_HIDDEN_CHUNK),
        grid=(b, l // tm),
        in_specs=[
            pl.BlockSpec((1, tm, d), lambda i, j: (i, j, 0)),
            pl.BlockSpec((1, 3, d), lambda i, j: (i, 0, 0)),
            pl.BlockSpec((1, d), const),
            pl.BlockSpec((1, d), const),
            pl.BlockSpec((d, 2 * hidden), const, pipeline_mode=pl.Buffered(1)),
            pl.BlockSpec((hidden, d), const, pipeline_mode=pl.Buffered(1)),
        ],
        out_specs=pl.BlockSpec((1, tm, d), lambda i, j: (i, j, 0)),
        out_shape=jax.ShapeDtypeStruct((b, l, d), jnp.float32),
        scratch_shapes=[pltpu.VMEM((tm, d), jnp.float32)],
        compiler_params=pltpu.CompilerParams(
            dimension_semantics=("arbitrary", "arbitrary"), vmem_limit_bytes=VMEM_LIMIT_BYTES),
        name="ffn_sublayer",
    )(x, mods, g_pre.reshape(1, d), g_post.reshape(1, d), w_in, w_out)


def rmsnorm(x, g):
    xf = x.astype(jnp.float32)
    y = xf * lax.rsqrt(jnp.mean(xf * xf, axis=-1, keepdims=True) + EPS)
    return (y * g.astype(jnp.float32)).astype(x.dtype)


def modulated_norm(x, g, shift, scale):
    return rmsnorm(x, g) * (1.0 + scale) + shift


def add_residual(x, y, g_post, gate, weight):
    return x + weight * gate * rmsnorm(y, g_post)


def heads(a, n, d):
    b, l, _ = a.shape
    return a.astype(jnp.float32).reshape(b, l, n, d)


def _to_chunks(a):
    b, l, h, d = a.shape
    return a.reshape(b, l // CHUNK, CHUNK, h, d).transpose(1, 0, 3, 2, 4)


def _from_chunks(a):
    n, b, h, c, d = a.shape
    return a.transpose(1, 0, 3, 2, 4).reshape(b, n * c, h, d)


def chunked_gated_recurrence(q, k, v, log_f, s0):
    mask = jnp.tril(jnp.ones((CHUNK, CHUNK), dtype=bool))[:, :, None]

    def step(s, inp):
        qc, kc, vc, gc = inp
        g = jnp.cumsum(gc, axis=2)
        diff = g[:, :, :, None, :] - g[:, :, None, :, :]
        decay = jnp.exp(jnp.where(mask, diff, -jnp.inf))
        scores = jnp.einsum('bhtc,bhsc,bhtsc->bhts', qc, kc, decay)
        o = (jnp.einsum('bhts,bhsv->bhtv', scores, vc)
             + jnp.einsum('bhtc,bhcv->bhtv', qc * jnp.exp(g), s))
        g_last = g[:, :, -1:, :]
        s = (jnp.exp(g_last[:, :, 0, :, None]) * s
             + jnp.einsum('bhsc,bhsv->bhcv', kc * jnp.exp(g_last - g), vc))
        return s, o

    s_final, o = lax.scan(step, s0, (_to_chunks(q), _to_chunks(k), _to_chunks(v), _to_chunks(log_f)))
    return _from_chunks(o), s_final


def bidir_recurrence(q, k_f, k_b, v, lf_f, lf_b, s0_f, s0_b):
    o_f, s_f = chunked_gated_recurrence(q, k_f, v, lf_f, s0_f)
    flip = lambda a: jnp.flip(a, axis=1)
    o_b, s_b = chunked_gated_recurrence(flip(q), flip(k_b), flip(v), flip(lf_b), s0_b)
    return o_f + flip(o_b), s_f, s_b


def gated_head_norm(o, gnorm, g):
    b, l = o.shape[:2]
    y = rmsnorm(o, gnorm).reshape(b, l, -1) * jax.nn.silu(g.astype(jnp.float32))
    return y.astype(g.dtype)


def lower_bound(logits, layer):
    cum = jnp.cumsum(jax.nn.softmax(logits.astype(jnp.float32), axis=0), axis=0)
    return cum[layer] - cum[0]


def hgrn2_scan(hq, hf_f, hf_b, hi, lb_f, lb_b, s0_f, s0_b):
    q = heads(jax.nn.silu(hq), HG_HEADS, HG_DK) * HG_DK ** -0.5
    v = heads(hi, HG_HEADS, HG_DV)

    def gate(hf, lb):
        z = heads(hf, HG_HEADS, HG_DK)
        lb = lb.reshape(HG_HEADS, HG_DK)
        log_f = jnp.logaddexp(jnp.log(lb), jnp.log1p(-lb) + jax.nn.log_sigmoid(z))
        k = (1.0 - lb) * jax.nn.sigmoid(-z)
        return k, log_f

    k_f, lf_f = gate(hf_f, lb_f)
    k_b, lf_b = gate(hf_b, lb_b)
    return bidir_recurrence(q, k_f, k_b, v, lf_f, lf_b, s0_f, s0_b)


def gla_scan(gq, gk, gv, glr, gate_w, gate_b, s0_f, s0_b):
    q = heads(gq, GLA_HEADS, GLA_DK) * GLA_DK ** -0.5
    k = heads(gk, GLA_HEADS, GLA_DK)
    v = heads(gv, GLA_HEADS, GLA_DV)
    lr_f, lr_b = jnp.split(glr, 2, axis=-1)

    def gate(lr, d):
        z = heads(lr @ gate_w[d] + gate_b[d], GLA_HEADS, GLA_DK)
        return jax.nn.log_sigmoid(z) / GLA_GATE_NORM

    return bidir_recurrence(q, k, k, v, gate(lr_f, 0), gate(lr_b, 1), s0_f, s0_b)


def _linear_combine(e1, e2):
    a1, b1 = e1
    a2, b2 = e2
    return a1 * a2, a2 * b1 + b2


def s5_scan(u, a_re, a_im, log_dt, b_re, b_im, x0_f, x0_b):
    b, l, _ = u.shape
    uc = u.astype(jnp.float32).reshape(b, l, S5_GROUPS, S5_GROUP).astype(jnp.complex64)
    bmat = lax.complex(b_re.astype(jnp.float32), b_im.astype(jnp.float32))

    def direction(d, x0, reverse):
        lam = lax.complex(a_re[d].astype(jnp.float32), a_im[d].astype(jnp.float32))
        dt = jnp.exp(log_dt[d].astype(jnp.float32))[:, None]
        lam_bar = jnp.exp(lam * dt)
        b_bar = ((lam_bar - 1.0) / lam)[:, :, None] * bmat
        bu = jnp.einsum('blgh,gph->blgp', uc, b_bar)
        edge = -1 if reverse else 0
        bu = bu.at[:, edge].add(lam_bar * x0)
        a = jnp.broadcast_to(lam_bar, (1, l) + lam_bar.shape)
        _, xs = lax.associative_scan(_linear_combine, (a, bu), reverse=reverse, axis=1)
        return xs, xs[:, 0 if reverse else -1]

    xs_f, xf = direction(0, x0_f, False)
    xs_b, xb = direction(1, x0_b, True)
    return xs_f + xs_b, xf, xb


def s5_readout(xs, u, c_re, c_im, d, glu_w, glu_b):
    b, l, _ = u.shape
    cmat = lax.complex(c_re.astype(jnp.float32), c_im.astype(jnp.float32))
    y = jnp.einsum('blgp,ghp->blgh', xs, cmat).real.reshape(b, l, MIX_W)
    y = y + d.astype(jnp.float32) * u.astype(jnp.float32)
    y = jax.nn.gelu(y).astype(u.dtype)
    a, g = jnp.split(y @ glu_w + glu_b, 2, axis=-1)
    return a * jax.nn.sigmoid(g)


def depthwise_conv3(u, w, bias):
    y = lax.conv_general_dilated(u, w[:, None, :].astype(u.dtype), window_strides=(1,),
                                 padding=((1, 1),), dimension_numbers=('NWC', 'WIO', 'NWC'),
                                 feature_group_count=u.shape[-1])
    return y + bias


def hyena_filters(l, fw_in, fb_in, fw_mid, fb_mid, fw_out, freq):
    f32 = jnp.float32
    t = jnp.linspace(0.0, 1.0, l, dtype=f32)[:, None]
    pos = jnp.arange(l, dtype=f32)[:, None]
    bands = jnp.linspace(1e-4, HY_BANDS - 1, HY_BANDS, dtype=f32)
    ang = (2.0 * math.pi / l) * pos * bands
    feat = jnp.concatenate([t, jnp.cos(ang), -jnp.sin(ang)], axis=-1)
    w = freq.astype(f32)
    h = jnp.sin(w * (feat @ fw_in.astype(f32) + fb_in.astype(f32)))
    for j in range(HY_INNER):
        h = jnp.sin(w * (h @ fw_mid[j].astype(f32) + fb_mid[j].astype(f32)))
    h = h @ fw_out.astype(f32)
    deltas = jnp.abs(jnp.linspace(HY_MIN_DECAY, HY_MAX_DECAY, HY_ORDER * MIX_W, dtype=f32))
    h = h * jnp.exp(-t * deltas)
    h = h * lax.rsqrt(2.0 * jnp.sum(h * h, axis=0, keepdims=True) - h[:1] ** 2 + EPS)
    return h.reshape(l, HY_ORDER, MIX_W)


def long_conv(z, h, bias):
    l = z.shape[1]
    k = jnp.concatenate([h, jnp.zeros_like(h[:1]), jnp.flip(h[1:], axis=0)], axis=0)
    zf = jnp.fft.rfft(z, n=2 * l, axis=1)
    kf = jnp.fft.rfft(k, axis=0)
    y = jnp.fft.irfft(zf * kf[None], n=2 * l, axis=1)[:, :l]
    return y + bias * z


def hyena_mixer(u, rows, p):
    b, l, c3 = u.shape
    if rows is None:
        uc = depthwise_conv3(u, p['hy_conv_w'], p['hy_conv_b'])
    else:
        uc = depthwise_conv3(u.reshape(b * rows, GRID_W, c3), p['hy_conv_w'],
                             p['hy_conv_b']).reshape(b, l, c3)
    v, x1, x2 = jnp.split(uc, HY_ORDER + 1, axis=-1)
    filt = hyena_filters(l, p['hy_fw_in'], p['hy_fb_in'], p['hy_fw_mid'], p['hy_fb_mid'],
                         p['hy_fw_out'], p['hy_freq'])
    z = v.astype(jnp.float32)
    for n, gate in enumerate((x1, x2)):
        z = gate.astype(jnp.float32) * long_conv(z, filt[:, n], p['hy_bias'][n].astype(jnp.float32))
    return z.astype(u.dtype)


def zero_states(b):
    return (jnp.zeros((b, HG_HEADS, HG_DK, HG_DV), jnp.float32),
            jnp.zeros((b, HG_HEADS, HG_DK, HG_DV), jnp.float32),
            jnp.zeros((b, S5_GROUPS, S5_STATE), jnp.complex64),
            jnp.zeros((b, S5_GROUPS, S5_STATE), jnp.complex64),
            jnp.zeros((b, GLA_HEADS, GLA_DK, GLA_DV), jnp.float32),
            jnp.zeros((b, GLA_HEADS, GLA_DK, GLA_DV), jnp.float32))


def token_mixer(h, p, states, rows, with_output):
    b, l, _ = h.shape
    idx = [int(i) for i in np.cumsum(IN_SPLIT)[:-1]]
    (hq, hf_f, hf_b, hi, hg, s5_u, hy_u, gq, gk, gv, gg, glr, gates) = jnp.split(
        h @ p['w_in'], idx, axis=-1)
    s_hg_f, s_hg_b, x_s5_f, x_s5_b, s_gla_f, s_gla_b = states
    o_hg, s_hg_f, s_hg_b = hgrn2_scan(hq, hf_f, hf_b, hi, p['hg_lb_fwd'], p['hg_lb_bwd'],
                                      s_hg_f, s_hg_b)
    xs_s5, x_s5_f, x_s5_b = s5_scan(s5_u, p['s5_a_re'], p['s5_a_im'], p['s5_log_dt'],
                                    p['s5_b_re'], p['s5_b_im'], x_s5_f, x_s5_b)
    o_gla, s_gla_f, s_gla_b = gla_scan(gq, gk, gv, glr, p['gla_gate_w'], p['gla_gate_b'],
                                       s_gla_f, s_gla_b)
    new_states = (s_hg_f, s_hg_b, x_s5_f, x_s5_b, s_gla_f, s_gla_b)
    if not with_output:
        return None, new_states
    branches = (gated_head_norm(o_hg, p['hg_gnorm'], hg),
                s5_readout(xs_s5, s5_u, p['s5_c_re'], p['s5_c_im'], p['s5_d'],
                           p['s5_glu_w'], p['s5_glu_b']),
                hyena_mixer(hy_u, rows, p),
                gated_head_norm(o_gla, p['gla_gnorm'], gg))
    gates = jax.nn.sigmoid(gates.reshape(b, l, N_BRANCH, D_MODEL))
    merged = gates[:, :, 0] * (branches[0] @ p['w_branch'][0])
    for n in range(1, N_BRANCH):
        merged = merged + gates[:, :, n] * (branches[n] @ p['w_branch'][n])
    return merged @ p['w_out'], new_states


def kernel(x, c, ctx, c_ctx, ada_w, ada_b, norm_pre, norm_post, ffn1_w_in, ffn1_w_out,
           ffn2_w_in, ffn2_w_out, w_in, hg_lb_fwd, hg_lb_bwd, hg_gnorm, s5_a_re, s5_a_im,
           s5_log_dt, s5_b_re, s5_b_im, s5_c_re, s5_c_im, s5_d, s5_glu_w, s5_glu_b,
           hy_conv_w, hy_conv_b, hy_fw_in, hy_fb_in, hy_fw_mid, hy_fb_mid, hy_fw_out,
           hy_freq, hy_bias, gla_gate_w, gla_gate_b, gla_gnorm, w_branch, w_out):
    b = x.shape[0]
    depth = ada_w.shape[0]
    rows = x.shape[1] // GRID_W
    d = x.shape[-1]
    c_act = jax.nn.silu(c)
    cc_act = jax.nn.silu(c_ctx)
    bf16 = jnp.bfloat16
    xc = ctx
    for l in range(depth):
        last = l == depth - 1
        p = dict(w_in=w_in[l], hg_lb_fwd=lower_bound(hg_lb_fwd, l),
                 hg_lb_bwd=lower_bound(hg_lb_bwd, l), hg_gnorm=hg_gnorm[l],
                 s5_a_re=s5_a_re[l], s5_a_im=s5_a_im[l], s5_log_dt=s5_log_dt[l],
                 s5_b_re=s5_b_re[l], s5_b_im=s5_b_im[l], s5_c_re=s5_c_re[l], s5_c_im=s5_c_im[l],
                 s5_d=s5_d[l], s5_glu_w=s5_glu_w[l], s5_glu_b=s5_glu_b[l],
                 hy_conv_w=hy_conv_w[l], hy_conv_b=hy_conv_b[l], hy_fw_in=hy_fw_in[l],
                 hy_fb_in=hy_fb_in[l], hy_fw_mid=hy_fw_mid[l], hy_fb_mid=hy_fb_mid[l],
                 hy_fw_out=hy_fw_out[l], hy_freq=hy_freq[l], hy_bias=hy_bias[l],
                 gla_gate_w=gla_gate_w[l], gla_gate_b=gla_gate_b[l], gla_gnorm=gla_gnorm[l],
                 w_branch=w_branch[l], w_out=w_out[l])
        mod_all = (c_act @ ada_w[l] + ada_b[l]).reshape(b, N_MOD, d)
        modc_all = jnp.broadcast_to((cc_act @ ada_w[l] + ada_b[l]).reshape(1, N_MOD, d), (b, N_MOD, d))
        mod = [mod_all[:, i:i + 1, :] for i in range(N_MOD)]
        modc = [modc_all[0, i] for i in range(N_MOD)]
        w1i, w1o = ffn1_w_in[l].astype(bf16), ffn1_w_out[l].astype(bf16)
        w2i, w2o = ffn2_w_in[l].astype(bf16), ffn2_w_out[l].astype(bf16)
        x = ffn_sublayer(x, mod_all[:, 0:3], norm_pre[l, 0], norm_post[l, 0], w1i, w1o)
        xc = ffn_sublayer(xc, modc_all[:, 0:3], norm_pre[l, 0], norm_post[l, 0], w1i, w1o)
        hc = modulated_norm(xc, norm_pre[l, 1], modc[3], modc[4])
        hx = modulated_norm(x, norm_pre[l, 1], mod[3], mod[4])
        yc, ctx_states = token_mixer(hc, p, zero_states(b), None, not last)
        yx, _ = token_mixer(hx, p, ctx_states, rows, True)
        x = add_residual(x, yx, norm_post[l, 1], mod[5], 1.0)
        x = ffn_sublayer(x, mod_all[:, 6:9], norm_pre[l, 2], norm_post[l, 2], w2i, w2o)
        if not last:
            xc = add_residual(xc, yc, norm_post[l, 1], modc[5], 1.0)
            xc = ffn_sublayer(xc, modc_all[:, 6:9], norm_pre[l, 2], norm_post[l, 2], w2i, w2o)
    return x
```

```python
import functools
import math

import jax
import jax.numpy as jnp
import numpy as np
from jax import lax
from jax.experimental import pallas as pl
from jax.experimental.pallas import tpu as pltpu

D_MODEL = 1024
GRID_W = 64
N_MOD = 9
EPS = 1e-6
MIX_W = 512
N_BRANCH = 4
HG_HEADS = 4
HG_DK = 128
HG_DV = 128
S5_GROUP = 16
S5_GROUPS = MIX_W // S5_GROUP
S5_STATE = 64
HY_ORDER = 2
HY_BANDS = 16
HY_INNER = 2
HY_MIN_DECAY = math.log(1e-2) / 1.5
HY_MAX_DECAY = math.log(1e-2) / 0.3
GLA_HEADS = 4
GLA_DK = 64
GLA_DV = 128
GLA_RANK = 16
GLA_GATE_NORM = 16.0
HG_KEY = HG_HEADS * HG_DK
GLA_KEY = GLA_HEADS * GLA_DK

LANES = 128
MXU_DIM = 256
VMEM_LIMIT_BYTES = 56 * 1024 * 1024
FFN_TOKEN_TILE = 512
FFN_HIDDEN_CHUNK = MXU_DIM
PROJ_TOKEN_TILE = 512
MERGE_TOKEN_TILE = 256
REC_CHUNK = 64
REC_BLOCK = 512
REC_EXP_CLAMP = 80.0
S5_CHUNK = 64
FFT_N2 = LANES
FFT_K1_TILE = 4
FFT_COL_TILE = 4096

F32 = jnp.float32
BF16 = jnp.bfloat16
HIGHEST = lax.Precision.HIGHEST


def _cparams(n_axes):
    return pltpu.CompilerParams(dimension_semantics=("arbitrary",) * n_axes,
                                vmem_limit_bytes=VMEM_LIMIT_BYTES)


def _sigmoid(x):
    return 1.0 / (1.0 + jnp.exp(-x))


def _log_sigmoid(x):
    return -(jnp.maximum(-x, 0.0) + jnp.log(1.0 + jnp.exp(-jnp.abs(x))))


def _ffn_body(x_ref, mod_ref, gpre_ref, gpost_ref, win_ref, wout_ref, o_ref, acc_ref, *, hidden, chunk):
    x = x_ref[0]
    xn = x * lax.rsqrt(jnp.mean(x * x, axis=-1, keepdims=True) + EPS) * gpre_ref[...]
    shift = mod_ref[0, 0:1, :]
    scale = mod_ref[0, 1:2, :]
    gate = mod_ref[0, 2:3, :]
    h = (xn * (1.0 + scale) + shift).astype(BF16)
    for j in range(hidden // chunk):
        a = jnp.dot(h, win_ref[:, j * chunk:(j + 1) * chunk], preferred_element_type=F32)
        b = jnp.dot(h, win_ref[:, hidden + j * chunk:hidden + (j + 1) * chunk],
                    preferred_element_type=F32)
        g = (a * _sigmoid(a) * b).astype(BF16)
        part = jnp.dot(g, wout_ref[j * chunk:(j + 1) * chunk, :], preferred_element_type=F32)
        if j == 0:
            acc_ref[...] = part
        else:
            acc_ref[...] += part
    y = acc_ref[...]
    yn = y * lax.rsqrt(jnp.mean(y * y, axis=-1, keepdims=True) + EPS) * gpost_ref[...]
    o_ref[0] = x + 0.5 * gate * yn


def ffn_sublayer(x, mods, g_pre, g_post, w_in, w_out):
    b, l, d = x.shape
    hidden = w_out.shape[0]
    tm = min(FFN_TOKEN_TILE, l)
    assert l % tm == 0 and hidden % FFN_HIDDEN_CHUNK == 0
    const = lambda i, j: (0, 0)
    return pl.pallas_call(
        functools.partial(_ffn_body, hidden=hidden, chunk=FFN_HIDDEN_CHUNK),
        grid=(b, l // tm),
        in_specs=[
            pl.BlockSpec((1, tm, d), lambda i, j: (i, j, 0)),
            pl.BlockSpec((1, 3, d), lambda i, j: (i, 0, 0)),
            pl.BlockSpec((1, d), const),
            pl.BlockSpec((1, d), const),
            pl.BlockSpec((d, 2 * hidden), const, pipeline_mode=pl.Buffered(1)),
            pl.BlockSpec((hidden, d), const, pipeline_mode=pl.Buffered(1)),
        ],
        out_specs=pl.BlockSpec((1, tm, d), lambda i, j: (i, j, 0)),
        out_shape=jax.ShapeDtypeStruct((b, l, d), F32),
        scratch_shapes=[pltpu.VMEM((tm, d), F32)],
        compiler_params=_cparams(2),
        name="ffn_sublayer",
    )(x, mods, g_pre.reshape(1, d), g_post.reshape(1, d), w_in, w_out)


def _mm_body(*refs, n_extra, epilogue):
    a_ref, b_ref = refs[0], refs[1]
    extras = refs[2:2 + n_extra]
    o_ref = refs[2 + n_extra]
    acc = jnp.dot(a_ref[0], b_ref[0], preferred_element_type=F32)
    if epilogue is not None:
        acc = epilogue(acc, *extras)
    o_ref[0] = acc.astype(o_ref.dtype)


def matmul(a, b, *, out_dtype, tm, tn, epilogue=None, out_cols=None, extras=(), extra_specs=(), name):
    ga, m, k = a.shape
    gb, k2, n = b.shape
    assert k == k2 and m % tm == 0 and n % tn == 0
    g = max(ga, gb)
    oc = tn if out_cols is None else out_cols
    assert out_cols is None or tn == n
    a_map = (lambda gi, i, j: (gi, i, 0)) if ga > 1 else (lambda gi, i, j: (0, i, 0))
    b_map = (lambda gi, i, j: (gi, 0, j)) if gb > 1 else (lambda gi, i, j: (0, 0, j))
    return pl.pallas_call(
        functools.partial(_mm_body, n_extra=len(extras), epilogue=epilogue),
        grid=(g, m // tm, n // tn),
        in_specs=[pl.BlockSpec((1, tm, k), a_map), pl.BlockSpec((1, k, tn), b_map), *extra_specs],
        out_specs=pl.BlockSpec((1, tm, oc), lambda gi, i, j: (gi, i, j)),
        out_shape=jax.ShapeDtypeStruct((g, m, (n // tn) * oc), out_dtype),
        compiler_params=_cparams(3),
        name=name,
    )(a, b, *extras)


def _rec_core(d, load_qkl, v_ref, s0_ref, o_ref, sfin_ref, st_ref, *, heads, dk, dv, nchunks):
    c_len = REC_CHUNK
    j = pl.program_id(2)

    @pl.when(j == 0)
    def _():
        st_ref[...] = s0_ref[0, 0]

    row = lax.broadcasted_iota(jnp.int32, (c_len, c_len), 0)
    col = lax.broadcasted_iota(jnp.int32, (c_len, c_len), 1)
    sign = 1 - 2 * d
    tri = (row - col) * sign >= 0
    trif = tri.astype(F32)

    def chunk(ci, carry):
        c = ci + d * (nchunks - 1 - 2 * ci)
        r0 = pl.multiple_of(c * c_len, c_len)
        q, k, lf = load_qkl(r0)
        v = v_ref[0, pl.ds(r0, c_len), :].astype(F32)
        g = jnp.dot(trif, lf, precision=HIGHEST, preferred_element_type=F32)
        gtot = jnp.sum(lf, axis=0, keepdims=True)
        gmid = g[c_len // 2:c_len // 2 + 1, :]
        qs = (q * jnp.exp(g)).astype(BF16)
        ks = (k * jnp.exp(gtot - g)).astype(BF16)
        qh = (q * jnp.exp(jnp.minimum(g - gmid, REC_EXP_CLAMP))).astype(BF16)
        kh = (k * jnp.exp(jnp.minimum(gmid - g, REC_EXP_CLAMP))).astype(BF16)
        dec = jnp.exp(gtot)
        outs = []
        for h in range(heads):
            ksl = slice(h * dk, (h + 1) * dk)
            vh = v[:, h * dv:(h + 1) * dv]
            vhb = vh.astype(BF16)
            sc = lax.dot_general(qh[:, ksl], kh[:, ksl], (((1,), (1,)), ((), ())),
                                 preferred_element_type=F32)
            sc = jnp.where(tri, sc, 0.0).astype(BF16)
            st = st_ref[h]
            o = jnp.dot(sc, vhb, preferred_element_type=F32)
            o = o + lax.dot_general(qs[:, ksl], st.astype(BF16), (((1,), (1,)), ((), ())),
                                    preferred_element_type=F32)
            upd = jnp.dot(vh.T.astype(BF16), ks[:, ksl], preferred_element_type=F32)
            st_ref[h] = st * dec[:, ksl] + upd
            outs.append(o)
        o_ref[0, 0, pl.ds(r0, c_len), :] = jnp.concatenate(outs, axis=1)
        return carry

    lax.fori_loop(0, nchunks, chunk, 0)

    @pl.when(j == pl.num_programs(2) - 1)
    def _():
        sfin_ref[0, 0] = st_ref[...]


def _hg_body(hq_ref, hf_ref, v_ref, gp_ref, s0_ref, o_ref, sfin_ref, st_ref, *, nchunks):
    d = pl.program_id(1)
    log_lb = gp_ref[0, 0:1, :]
    log_1mlb = gp_ref[0, 1:2, :]
    one_mlb = gp_ref[0, 2:3, :]

    def load_qkl(r0):
        hq = hq_ref[0, pl.ds(r0, REC_CHUNK), :]
        z = hf_ref[0, pl.ds(r0, REC_CHUNK), :]
        q = hq * _sigmoid(hq) * (HG_DK ** -0.5)
        k = one_mlb * _sigmoid(-z)
        b = log_1mlb + _log_sigmoid(z)
        lf = jnp.maximum(log_lb, b) + jnp.log(1.0 + jnp.exp(-jnp.abs(log_lb - b)))
        return q, k, lf

    _rec_core(d, load_qkl, v_ref, s0_ref, o_ref, sfin_ref, st_ref,
              heads=HG_HEADS, dk=HG_DK, dv=HG_DV, nchunks=nchunks)


def _gla_body(q_ref, k_ref, v_ref, lr_ref, gw_ref, gb_ref, s0_ref, o_ref, sfin_ref, st_ref, *, nchunks):
    d = pl.program_id(1)

    def load_qkl(r0):
        q = q_ref[0, pl.ds(r0, REC_CHUNK), :] * (GLA_DK ** -0.5)
        k = k_ref[0, pl.ds(r0, REC_CHUNK), :]
        lr = lr_ref[0, pl.ds(r0, REC_CHUNK), :]
        z = jnp.dot(lr, gw_ref[0], precision=HIGHEST, preferred_element_type=F32) + gb_ref[0]
        return q, k, _log_sigmoid(z) * (1.0 / GLA_GATE_NORM)

    _rec_core(d, load_qkl, v_ref, s0_ref, o_ref, sfin_ref, st_ref,
              heads=GLA_HEADS, dk=GLA_DK, dv=GLA_DV, nchunks=nchunks)


def _rec_geometry(l):
    t = min(REC_BLOCK, l)
    assert l % t == 0 and t % REC_CHUNK == 0
    return t, l // t


def _rec_row_block(nb):
    return lambda d, j: j + d * (nb - 1 - 2 * j)


def hgrn2_recurrence(proj, gate_params, s0):
    b, l, _ = proj.shape
    t, nb = _rec_geometry(l)
    rb = _rec_row_block(nb)
    w = HG_KEY
    st_shape = (HG_HEADS, HG_DV, HG_DK)
    return pl.pallas_call(
        functools.partial(_hg_body, nchunks=t // REC_CHUNK),
        grid=(b, 2, nb),
        in_specs=[
            pl.BlockSpec((1, t, w), lambda i, d, j: (i, rb(d, j), 0)),
            pl.BlockSpec((1, t, w), lambda i, d, j: (i, rb(d, j), 1 + d)),
            pl.BlockSpec((1, t, w), lambda i, d, j: (i, rb(d, j), 3)),
            pl.BlockSpec((1, 3, w), lambda i, d, j: (d, 0, 0)),
            pl.BlockSpec((1, 1) + st_shape, lambda i, d, j: (i, d, 0, 0, 0)),
        ],
        out_specs=[
            pl.BlockSpec((1, 1, t, w), lambda i, d, j: (d, i, rb(d, j), 0)),
            pl.BlockSpec((1, 1) + st_shape, lambda i, d, j: (i, d, 0, 0, 0)),
        ],
        out_shape=[jax.ShapeDtypeStruct((2, b, l, w), F32),
                   jax.ShapeDtypeStruct((b, 2) + st_shape, F32)],
        scratch_shapes=[pltpu.VMEM(st_shape, F32)],
        compiler_params=_cparams(3),
        name="hgrn2_recurrence",
    )(proj, proj, proj, gate_params, s0)


GLA_PROJ_COLS = 2 * GLA_KEY + 2 * MIX_W + LANES


def gla_recurrence(proj, gate_w, gate_b, s0):
    b, l, _ = proj.shape
    t, nb = _rec_geometry(l)
    rb = _rec_row_block(nb)
    st_shape = (GLA_HEADS, GLA_DV, GLA_DK)
    lr_block = (2 * GLA_KEY + 2 * MIX_W) // LANES
    return pl.pallas_call(
        functools.partial(_gla_body, nchunks=t // REC_CHUNK),
        grid=(b, 2, nb),
        in_specs=[
            pl.BlockSpec((1, t, GLA_KEY), lambda i, d, j: (i, rb(d, j), 0)),
            pl.BlockSpec((1, t, GLA_KEY), lambda i, d, j: (i, rb(d, j), 1)),
            pl.BlockSpec((1, t, MIX_W), lambda i, d, j: (i, rb(d, j), 1)),
            pl.BlockSpec((1, t, LANES), lambda i, d, j: (i, rb(d, j), lr_block)),
            pl.BlockSpec((1, LANES, GLA_KEY), lambda i, d, j: (d, 0, 0)),
            pl.BlockSpec((1, 1, GLA_KEY), lambda i, d, j: (d, 0, 0)),
            pl.BlockSpec((1, 1) + st_shape, lambda i, d, j: (i, d, 0, 0, 0)),
        ],
        out_specs=[
            pl.BlockSpec((1, 1, t, MIX_W), lambda i, d, j: (d, i, rb(d, j), 0)),
            pl.BlockSpec((1, 1) + st_shape, lambda i, d, j: (i, d, 0, 0, 0)),
        ],
        out_shape=[jax.ShapeDtypeStruct((2, b, l, MIX_W), F32),
                   jax.ShapeDtypeStruct((b, 2) + st_shape, F32)],
        scratch_shapes=[pltpu.VMEM(st_shape, F32)],
        compiler_params=_cparams(3),
        name="gla_recurrence",
    )(proj, proj, proj, proj, gate_w, gate_b, s0)


def _s5_scan_body(e_ref, la_ref, lb_ref, x0_ref, xs_ref, xfin_ref, *, nchunks):
    d = pl.program_id(0)
    la = la_ref[0]
    lb = lb_ref[0]

    def step(kk, x):
        c = kk + d * (nchunks - 1 - 2 * kk)
        xs_ref[0, c] = x
        return la * x + lb * pltpu.roll(x, LANES // 2, axis=1) + e_ref[0, c]

    xfin_ref[0] = lax.fori_loop(0, nchunks, step, x0_ref[0])


def s5_chunk_scan(e, lam_a, lam_b, x0):
    _, nc, rows, _ = e.shape
    g = lam_a.shape[1]
    return pl.pallas_call(
        functools.partial(_s5_scan_body, nchunks=nc),
        grid=(2, rows // g),
        in_specs=[
            pl.BlockSpec((1, nc, g, LANES), lambda d, i: (d, 0, i, 0)),
            pl.BlockSpec((1, g, LANES), lambda d, i: (d, 0, 0)),
            pl.BlockSpec((1, g, LANES), lambda d, i: (d, 0, 0)),
            pl.BlockSpec((1, g, LANES), lambda d, i: (d, i, 0)),
        ],
        out_specs=[
            pl.BlockSpec((1, nc, g, LANES), lambda d, i: (d, 0, i, 0)),
            pl.BlockSpec((1, g, LANES), lambda d, i: (d, i, 0)),
        ],
        out_shape=[jax.ShapeDtypeStruct(e.shape, F32), jax.ShapeDtypeStruct(x0.shape, F32)],
        compiler_params=_cparams(2),
        name="s5_chunk_scan",
    )(e, lam_a, lam_b, x0)


def s5_matrices(a_re, a_im, log_dt, b_re, b_im, c_re, c_im):
    c_len = S5_CHUNK
    hp = dict(precision=HIGHEST)
    lam = lax.complex(a_re.astype(F32), a_im.astype(F32))
    dt = jnp.exp(log_dt.astype(F32))[:, :, None]
    lam_dt = lam * dt
    lam_bar = jnp.exp(lam_dt)
    bmat = lax.complex(b_re.astype(F32), b_im.astype(F32))
    cmat = lax.complex(c_re.astype(F32), c_im.astype(F32))
    b_bar = ((lam_bar - 1.0) / lam)[:, :, :, None] * bmat[None]
    tau = jnp.arange(c_len + 1, dtype=F32)[None, :, None, None]
    pw = jnp.exp(lam_dt[:, None] * tau)
    kern = jnp.einsum('ghp,dtgp,dgpk->dtghk', cmat, pw[:, :c_len], b_bar, **hp).real
    k_all = jnp.concatenate([jnp.flip(kern[1, 1:], axis=0), (kern[0, 0] + kern[1, 0])[None], kern[0, 1:]],
                            axis=0)
    i_idx = jnp.arange(c_len)
    lag = i_idx[None, :] - i_idx[:, None] + (c_len - 1)
    m5 = k_all[lag]
    m_mat = m5.transpose(2, 0, 4, 1, 3).reshape(S5_GROUPS, c_len * S5_GROUP, c_len * S5_GROUP)
    qf = pw[0, c_len - 1 - i_idx][:, :, :, None] * b_bar[0][None]
    qb = pw[1, i_idx][:, :, :, None] * b_bar[1][None]
    q4 = jnp.stack([qf.real, qf.imag, qb.real, qb.imag], axis=0)
    q_mat = q4.transpose(2, 1, 4, 0, 3).reshape(S5_GROUPS, c_len * S5_GROUP, 4 * S5_STATE)
    wf = cmat[None] * pw[0, 1 + i_idx][:, :, None, :]
    wb = cmat[None] * pw[1, c_len - i_idx][:, :, None, :]
    p4 = jnp.stack([wf.real, -wf.imag, wb.real, -wb.imag], axis=0)
    p_mat = p4.transpose(2, 0, 4, 1, 3).reshape(S5_GROUPS, 4 * S5_STATE, c_len * S5_GROUP)
    mp_mat = jnp.concatenate([m_mat, p_mat], axis=1)
    lc = pw[:, c_len]
    lam_a = jnp.concatenate([lc.real, lc.real], axis=-1)
    lam_b = jnp.concatenate([-lc.imag, lc.imag], axis=-1)
    return q_mat.astype(BF16), mp_mat.astype(BF16), lam_a, lam_b


def s5_mixer(u, mats, x0, with_output):
    q_mat, mp_mat, lam_a, lam_b = mats
    b, l, _ = u.shape
    c_len = S5_CHUNK
    nc = l // c_len
    rows = b * nc
    ug = u.reshape(b, nc, c_len, S5_GROUPS, S5_GROUP).transpose(3, 0, 1, 2, 4).reshape(
        S5_GROUPS, rows, c_len * S5_GROUP)
    tm = min(rows, 512)
    e = matmul(ug, q_mat, out_dtype=F32, tm=tm, tn=4 * S5_STATE, name="s5_local_states")
    e = e.reshape(S5_GROUPS, b, nc, 2, LANES).transpose(3, 2, 1, 0, 4).reshape(2, nc, b * S5_GROUPS, LANES)
    xs, xfin = s5_chunk_scan(e, lam_a, lam_b, x0)
    if not with_output:
        return None, xfin
    xg = xs.reshape(2, nc, b, S5_GROUPS, LANES).transpose(3, 2, 1, 0, 4).reshape(
        S5_GROUPS, rows, 2 * LANES).astype(BF16)
    y = matmul(jnp.concatenate([ug, xg], axis=-1), mp_mat, out_dtype=F32, tm=tm,
               tn=c_len * S5_GROUP, name="s5_outputs")
    y = y.reshape(S5_GROUPS, b, nc, c_len, S5_GROUP).transpose(1, 2, 3, 0, 4).reshape(b, l, MIX_W)
    return y, xfin


def _dft_tables(l):
    n = 2 * l
    n1 = n // FFT_N2
    k1 = np.arange(n1)[:, None]
    m1 = np.arange(n1 // 2)[None, :]
    th = 2.0 * np.pi * k1 * m1 / n1
    f_fwd = np.concatenate([np.cos(th), -np.sin(th)], axis=0)
    f_inv = np.concatenate([np.cos(th).T, -np.sin(th).T], axis=1) / n
    ph = 2.0 * np.pi * np.arange(n1)[:, None] * np.arange(FFT_N2)[None, :] / n
    tw = np.stack([np.cos(ph), np.sin(ph)], axis=0)[..., None]
    a2 = 2.0 * np.pi * np.arange(FFT_N2)[:, None] * np.arange(FFT_N2)[None, :] / FFT_N2
    wr, wi = np.cos(a2), -np.sin(a2)
    w_fwd = np.block([[wr, -wi], [wi, wr]])
    w_inv = np.block([[wr, wi], [-wi, wr]])
    return (jnp.asarray(f_fwd[None], BF16), jnp.asarray(f_inv[None], BF16), jnp.asarray(tw, F32),
            jnp.asarray(w_fwd, BF16), jnp.asarray(w_inv, BF16))


def _inner_dft(a_ref, tw_ref, wf_ref, kk):
    ar = a_ref[0, 0, kk].astype(F32)
    ai = a_ref[0, 1, kk].astype(F32)
    c = tw_ref[0, kk]
    s = tw_ref[1, kk]
    stack = jnp.concatenate([ar * c + ai * s, ai * c - ar * s], axis=0).astype(BF16)
    return jnp.dot(wf_ref[...], stack, preferred_element_type=F32), c, s


def _spectral_filter_body(a_ref, tw_ref, wf_ref, wi_ref, kf_ref, o_ref):
    for kk in range(FFT_K1_TILE):
        x, c, s = _inner_dft(a_ref, tw_ref, wf_ref, kk)
        kf = kf_ref[0, kk]
        y = jnp.concatenate([x[:FFT_N2] * kf, x[FFT_N2:] * kf], axis=0).astype(BF16)
        bm = jnp.dot(wi_ref[...], y, preferred_element_type=F32)
        br, bi = bm[:FFT_N2], bm[FFT_N2:]
        o_ref[0, 0, kk] = (br * c - bi * s).astype(o_ref.dtype)
        o_ref[0, 1, kk] = (bi * c + br * s).astype(o_ref.dtype)


def _filter_spectrum_body(a_ref, tw_ref, wf_ref, h0_ref, o_ref):
    for kk in range(FFT_K1_TILE):
        x, _, _ = _inner_dft(a_ref, tw_ref, wf_ref, kk)
        o_ref[0, kk] = 2.0 * x[:FFT_N2] - h0_ref[0]


def _outer_dft(z, f_fwd):
    g, l, c = z.shape
    n1 = 2 * l // FFT_N2
    cols = FFT_N2 * c
    zc = z.astype(BF16).reshape(g, n1 // 2, cols)
    a = matmul(f_fwd, zc, out_dtype=BF16, tm=2 * n1, tn=min(FFT_COL_TILE, cols), name="hyena_outer_dft")
    return a.reshape(g, 2, n1, FFT_N2, c)


def filter_spectrum(h, tables):
    f_fwd, _, tw, w_fwd, _ = tables
    g, l, c = h.shape
    n1 = 2 * l // FFT_N2
    a = _outer_dft(h, f_fwd)
    tk = FFT_K1_TILE
    return pl.pallas_call(
        _filter_spectrum_body,
        grid=(g, n1 // tk),
        in_specs=[
            pl.BlockSpec((1, 2, tk, FFT_N2, c), lambda i, j: (i, 0, j, 0, 0)),
            pl.BlockSpec((2, tk, FFT_N2, 1), lambda i, j: (0, j, 0, 0)),
            pl.BlockSpec((2 * FFT_N2, 2 * FFT_N2), lambda i, j: (0, 0)),
            pl.BlockSpec((1, 1, c), lambda i, j: (i, 0, 0)),
        ],
        out_specs=pl.BlockSpec((1, tk, FFT_N2, c), lambda i, j: (i, j, 0, 0)),
        out_shape=jax.ShapeDtypeStruct((g, n1, FFT_N2, c), F32),
        compiler_params=_cparams(2),
        name="hyena_filter_spectrum",
    )(a, tw, w_fwd, h[:, 0:1, :])


def long_conv_gated(z, kf, kf_sel, bias, gate, tables):
    f_fwd, f_inv, tw, w_fwd, w_inv = tables
    g, l, c = z.shape
    n1 = 2 * l // FFT_N2
    cols = FFT_N2 * c
    a = _outer_dft(z, f_fwd)
    tk = FFT_K1_TILE
    blk = (1, 2, tk, FFT_N2, c)
    bm = pl.pallas_call(
        _spectral_filter_body,
        grid=(g, n1 // tk),
        in_specs=[
            pl.BlockSpec(blk, lambda i, j: (i, 0, j, 0, 0)),
            pl.BlockSpec((2, tk, FFT_N2, 1), lambda i, j: (0, j, 0, 0)),
            pl.BlockSpec((2 * FFT_N2, 2 * FFT_N2), lambda i, j: (0, 0)),
            pl.BlockSpec((2 * FFT_N2, 2 * FFT_N2), lambda i, j: (0, 0)),
            pl.BlockSpec((1, tk, FFT_N2, c), lambda i, j: (kf_sel(i), j, 0, 0)),
        ],
        out_specs=pl.BlockSpec(blk, lambda i, j: (i, 0, j, 0, 0)),
        out_shape=jax.ShapeDtypeStruct((g, 2, n1, FFT_N2, c), BF16),
        compiler_params=_cparams(2),
        name="hyena_spectral_filter",
    )(a, tw, w_fwd, w_inv, kf)
    tn = min(FFT_COL_TILE, cols)
    bias_t = jnp.tile(bias, (1, FFT_N2)).reshape(bias.shape[0], 1, cols)

    def epilogue(acc, z_ref, gate_ref, bias_ref):
        return gate_ref[0] * (acc + bias_ref[0] * z_ref[0])

    row_spec = pl.BlockSpec((1, n1 // 2, tn), lambda gi, i, j: (gi, 0, j))
    y = matmul(f_inv, bm.reshape(g, 2 * n1, cols), out_dtype=F32, tm=n1 // 2, tn=tn, epilogue=epilogue,
               extras=(z.reshape(g, n1 // 2, cols), gate.reshape(g, n1 // 2, cols), bias_t),
               extra_specs=(row_spec, row_spec, pl.BlockSpec((1, 1, tn), lambda gi, i, j: (kf_sel(gi), 0, j))),
               name="hyena_outer_idft")
    return y.reshape(g, l, c)


def hyena_filters(l, fw_in, fb_in, fw_mid, fb_mid, fw_out, freq):
    hp = dict(precision=HIGHEST)
    t = jnp.linspace(0.0, 1.0, l, dtype=F32)[:, None]
    pos = jnp.arange(l, dtype=F32)[:, None]
    bands = jnp.linspace(1e-4, HY_BANDS - 1, HY_BANDS, dtype=F32)
    ang = (2.0 * math.pi / l) * pos * bands
    feat = jnp.concatenate([t, jnp.cos(ang), -jnp.sin(ang)], axis=-1)
    w = freq.astype(F32)
    h = jnp.sin(w * (jnp.dot(feat, fw_in.astype(F32), **hp) + fb_in.astype(F32)))
    for j in range(HY_INNER):
        h = jnp.sin(w * (jnp.dot(h, fw_mid[j].astype(F32), **hp) + fb_mid[j].astype(F32)))
    h = jnp.dot(h, fw_out.astype(F32), **hp)
    deltas = jnp.abs(jnp.linspace(HY_MIN_DECAY, HY_MAX_DECAY, HY_ORDER * MIX_W, dtype=F32))
    h = h * jnp.exp(-t * deltas)
    return h * lax.rsqrt(2.0 * jnp.sum(h * h, axis=0, keepdims=True) - h[:1] ** 2 + EPS)


def depthwise_conv3(u, w, bias, width):
    g, l, c = u.shape
    ur = u.reshape(g, l // width, width, c)
    zero = jnp.zeros_like(ur[:, :, :1])
    prev = jnp.concatenate([zero, ur[:, :, :-1]], axis=2)
    nxt = jnp.concatenate([ur[:, :, 1:], zero], axis=2)
    return (w[0] * prev + w[1] * ur + w[2] * nxt + bias).reshape(g, l, c)


def _merge_body(x_ref, h_ref, br_ref, mod_ref, gpost_ref, wg_ref, wb_ref, wo_ref, o_ref):
    d = x_ref.shape[-1]
    h = h_ref[0]
    merged = None
    for n in range(N_BRANCH):
        gate_n = _sigmoid(jnp.dot(h, wg_ref[:, n * d:(n + 1) * d], preferred_element_type=F32))
        t_n = jnp.dot(br_ref[n, 0], wb_ref[n], preferred_element_type=F32)
        merged = gate_n * t_n if merged is None else merged + gate_n * t_n
    y = jnp.dot(merged.astype(BF16), wo_ref[...], preferred_element_type=F32)
    yn = y * lax.rsqrt(jnp.mean(y * y, axis=-1, keepdims=True) + EPS) * gpost_ref[...]
    o_ref[0] = x_ref[0] + mod_ref[0] * yn


def merge_branches(x, h, branches, res_gate, g_post, w_gates, w_branch, w_out):
    b, l, d = x.shape
    tm = min(MERGE_TOKEN_TILE, l)
    const2 = lambda i, j: (0, 0)
    return pl.pallas_call(
        _merge_body,
        grid=(b, l // tm),
        in_specs=[
            pl.BlockSpec((1, tm, d), lambda i, j: (i, j, 0)),
            pl.BlockSpec((1, tm, d), lambda i, j: (i, j, 0)),
            pl.BlockSpec((N_BRANCH, 1, tm, MIX_W), lambda i, j: (0, i, j, 0)),
            pl.BlockSpec((1, 1, d), lambda i, j: (i, 0, 0)),
            pl.BlockSpec((1, d), const2),
            pl.BlockSpec((d, N_BRANCH * d), const2, pipeline_mode=pl.Buffered(1)),
            pl.BlockSpec((N_BRANCH, MIX_W, d), lambda i, j: (0, 0, 0), pipeline_mode=pl.Buffered(1)),
            pl.BlockSpec((d, d), const2, pipeline_mode=pl.Buffered(1)),
        ],
        out_specs=pl.BlockSpec((1, tm, d), lambda i, j: (i, j, 0)),
        out_shape=jax.ShapeDtypeStruct((b, l, d), F32),
        compiler_params=_cparams(2),
        name="merge_branches",
    )(x, h, branches, res_gate, g_post.reshape(1, d), w_gates, w_branch, w_out)


def _rmsnorm(x, g):
    return x * lax.rsqrt(jnp.mean(x * x, axis=-1, keepdims=True) + EPS) * g


def _project(h, w, out_dtype, name):
    b, l, d = h.shape
    n = w.shape[1]
    tm = min(PROJ_TOKEN_TILE, b * l)
    out = matmul(h.reshape(1, b * l, d), w[None], out_dtype=out_dtype, tm=tm, tn=n, name=name)
    return out.reshape(b, l, n)


def _gated_head_norm(o, gnorm, g, nheads, dv):
    _, b, l, _ = o.shape
    s = (o[0] + o[1]).reshape(b, l, nheads, dv)
    y = _rmsnorm(s, gnorm).reshape(b, l, nheads * dv)
    return (y * (g * _sigmoid(g))).astype(BF16)


def _lower_bound(logits, layer):
    cum = jnp.cumsum(jax.nn.softmax(logits.astype(F32), axis=0), axis=0)
    return cum[layer] - cum[0]


def _layer_params(l, w_in, hg_lb_fwd, hg_lb_bwd, s5_a_re, s5_a_im, s5_log_dt, s5_b_re, s5_b_im, s5_c_re,
                  s5_c_im, gla_gate_w, gla_gate_b):
    d = w_in.shape[1]
    cs = np.cumsum((0, HG_KEY, HG_KEY, HG_KEY, MIX_W, MIX_W, MIX_W, (HY_ORDER + 1) * MIX_W,
                    GLA_KEY, GLA_KEY, MIX_W, MIX_W, 2 * GLA_RANK, N_BRANCH * D_MODEL))
    w = w_in[l].astype(BF16)
    p = dict(
        w_hg=w[:, cs[0]:cs[5]], w_s5=w[:, cs[5]:cs[6]], w_hy=w[:, cs[6]:cs[7]],
        w_gla=jnp.pad(w[:, cs[7]:cs[12]], ((0, 0), (0, LANES - 2 * GLA_RANK))),
        w_gates=w[:, cs[12]:cs[13]])
    lb = jnp.stack([_lower_bound(hg_lb_fwd, l), _lower_bound(hg_lb_bwd, l)])
    p['hg_gate'] = jnp.stack([jnp.log(lb), jnp.log1p(-lb), 1.0 - lb], axis=1)
    gw = jnp.zeros((2, LANES, GLA_KEY), F32)
    gw = gw.at[0, :GLA_RANK].set(gla_gate_w[l, 0]).at[1, GLA_RANK:2 * GLA_RANK].set(gla_gate_w[l, 1])
    p['gla_gw'] = gw
    p['gla_gb'] = gla_gate_b[l].reshape(2, 1, GLA_KEY).astype(F32)
    p['s5'] = s5_matrices(s5_a_re[l], s5_a_im[l], s5_log_dt[l], s5_b_re[l], s5_b_im[l], s5_c_re[l], s5_c_im[l])
    return p


def _recurrent_branches(h, p, states):
    s_hg, s_gla, x_s5 = states
    pr_hg = _project(h, p['w_hg'], F32, "proj_hgrn2")
    pr_gla = _project(h, p['w_gla'], F32, "proj_gla")
    u_s5 = _project(h, p['w_s5'], BF16, "proj_s5")
    o_hg, s_hg = hgrn2_recurrence(pr_hg, p['hg_gate'], s_hg)
    o_gla, s_gla = gla_recurrence(pr_gla, p['gla_gw'], p['gla_gb'], s_gla)
    return pr_hg, pr_gla, u_s5, o_hg, o_gla, (s_hg, s_gla, x_s5)


def kernel(x, c, ctx, c_ctx, ada_w, ada_b, norm_pre, norm_post, ffn1_w_in, ffn1_w_out,
           ffn2_w_in, ffn2_w_out, w_in, hg_lb_fwd, hg_lb_bwd, hg_gnorm, s5_a_re, s5_a_im,
           s5_log_dt, s5_b_re, s5_b_im, s5_c_re, s5_c_im, s5_d, s5_glu_w, s5_glu_b,
           hy_conv_w, hy_conv_b, hy_fw_in, hy_fb_in, hy_fw_mid, hy_fb_mid, hy_fw_out,
           hy_freq, hy_bias, gla_gate_w, gla_gate_b, gla_gnorm, w_branch, w_out):
    b, seq, d = x.shape
    lc = ctx.shape[1]
    depth = ada_w.shape[0]
    hp = dict(precision=HIGHEST)
    c_act = c * _sigmoid(c)
    cc_act = c_ctx * _sigmoid(c_ctx)
    tables = _dft_tables(seq)
    xc = ctx
    for l in range(depth):
        last = l == depth - 1
        p = _layer_params(l, w_in, hg_lb_fwd, hg_lb_bwd, s5_a_re, s5_a_im, s5_log_dt, s5_b_re, s5_b_im,
                          s5_c_re, s5_c_im, gla_gate_w, gla_gate_b)
        mod = (jnp.dot(c_act, ada_w[l], **hp) + ada_b[l]).reshape(b, N_MOD, d)
        modc = jnp.broadcast_to((jnp.dot(cc_act, ada_w[l], **hp) + ada_b[l]).reshape(1, N_MOD, d), (b, N_MOD, d))
        w1i, w1o = ffn1_w_in[l].astype(BF16), ffn1_w_out[l].astype(BF16)
        w2i, w2o = ffn2_w_in[l].astype(BF16), ffn2_w_out[l].astype(BF16)
        x = ffn_sublayer(x, mod[:, 0:3], norm_pre[l, 0], norm_post[l, 0], w1i, w1o)
        xc = ffn_sublayer(xc, modc[:, 0:3], norm_pre[l, 0], norm_post[l, 0], w1i, w1o)
        hc = (_rmsnorm(xc, norm_pre[l, 1]) * (1.0 + modc[:, 4:5]) + modc[:, 3:4]).astype(BF16)
        hx = (_rmsnorm(x, norm_pre[l, 1]) * (1.0 + mod[:, 4:5]) + mod[:, 3:4]).astype(BF16)

        zero = (jnp.zeros((b, 2, HG_HEADS, HG_DV, HG_DK), F32), jnp.zeros((b, 2, GLA_HEADS, GLA_DV, GLA_DK), F32),
                jnp.zeros((2, b * S5_GROUPS, LANES), F32))
        c_hg, c_gla, c_u, c_ohg, c_ogla, st = _recurrent_branches(hc, p, zero)
        c_y5, c_x5 = s5_mixer(c_u, p['s5'], st[2], not last)
        st = (st[0], st[1], c_x5)
        x_hg, x_gla, x_u, x_ohg, x_ogla, st2 = _recurrent_branches(hx, p, st)
        x_y5, _ = s5_mixer(x_u, p['s5'], st[2], True)

        hy_x = _project(hx, p['w_hy'], F32, "proj_hyena")
        uc = depthwise_conv3(hy_x, hy_conv_w[l], hy_conv_b[l], GRID_W)
        filt = hyena_filters(seq, hy_fw_in[l], hy_fb_in[l], hy_fw_mid[l], hy_fb_mid[l], hy_fw_out[l], hy_freq[l])
        filt = filt.reshape(seq, HY_ORDER, MIX_W).transpose(1, 0, 2)
        if not last:
            hy_c = _project(hc, p['w_hy'], F32, "proj_hyena_ctx")
            ucc = depthwise_conv3(hy_c, hy_conv_w[l], hy_conv_b[l], lc)
            uc = jnp.concatenate([uc, jnp.pad(ucc, ((0, 0), (0, seq - lc), (0, 0)))], axis=0)
            filt_c = hyena_filters(lc, hy_fw_in[l], hy_fb_in[l], hy_fw_mid[l], hy_fb_mid[l], hy_fw_out[l],
                                   hy_freq[l]).reshape(lc, HY_ORDER, MIX_W).transpose(1, 0, 2)
            filt = jnp.concatenate([filt, jnp.pad(filt_c, ((0, 0), (0, seq - lc), (0, 0)))], axis=0)
        kf = filter_spectrum(filt, tables)
        z = uc[:, :, :MIX_W]
        for n in range(HY_ORDER):
            gate = uc[:, :, (n + 1) * MIX_W:(n + 2) * MIX_W]
            kf_n = kf[n::HY_ORDER]
            bias_n = jnp.broadcast_to(hy_bias[l, n].astype(F32), (kf_n.shape[0], MIX_W))
            z = long_conv_gated(z, kf_n, lambda i: i // b, bias_n, gate, tables)

        def finish(xres, hmix, pr_hg, pr_gla, u5, o_hg, o_gla, y5, zhy, m):
            br_hg = _gated_head_norm(o_hg, hg_gnorm[l], pr_hg[:, :, 4 * MIX_W:5 * MIX_W], HG_HEADS, HG_DV)
            br_gla = _gated_head_norm(o_gla, gla_gnorm[l], pr_gla[:, :, 2 * GLA_KEY + MIX_W:2 * GLA_KEY + 2 * MIX_W],
                                      GLA_HEADS, GLA_DV)
            y = jax.nn.gelu(y5 + s5_d[l].astype(F32) * u5.astype(F32)).astype(BF16)
            bb, ll, _ = y.shape

            def glu(acc, bias_ref):
                t = acc + bias_ref[...]
                return t[:, :MIX_W] * _sigmoid(t[:, MIX_W:])

            br_s5 = matmul(y.reshape(1, bb * ll, MIX_W), s5_glu_w[l].astype(BF16)[None], out_dtype=BF16,
                           tm=min(PROJ_TOKEN_TILE, bb * ll), tn=2 * MIX_W, epilogue=glu, out_cols=MIX_W,
                           extras=(s5_glu_b[l].reshape(1, 2 * MIX_W).astype(F32),),
                           extra_specs=(pl.BlockSpec((1, 2 * MIX_W), lambda gi, i, j: (0, 0)),),
                           name="s5_glu").reshape(bb, ll, MIX_W)
            branches = jnp.stack([br_hg, br_s5, zhy.astype(BF16), br_gla])
            return merge_branches(xres, hmix, branches, m[:, 5:6], norm_post[l, 1], p['w_gates'],
                                  w_branch[l].astype(BF16), w_out[l].astype(BF16))

        x = finish(x, hx, x_hg, x_gla, x_u, x_ohg, x_ogla, x_y5, z[:b], mod)
        x = ffn_sublayer(x, mod[:, 6:9], norm_pre[l, 2], norm_post[l, 2], w2i, w2o)
        if not last:
            xc = finish(xc, hc, c_hg, c_gla, c_u, c_ohg, c_ogla, c_y5, z[b:, :lc], modc)
            xc = ffn_sublayer(xc, modc[:, 6:9], norm_pre[l, 2], norm_post[l, 2], w2i, w2o)
    return x
```

```python
import functools
import math

import jax
import jax.numpy as jnp
import numpy as np
from jax import lax
from jax.experimental import pallas as pl
from jax.experimental.pallas import tpu as pltpu

D_MODEL = 1024
GRID_W = 64
N_MOD = 9
EPS = 1e-6
MIX_W = 512
N_BRANCH = 4
HG_HEADS = 4
HG_DK = 128
HG_DV = 128
S5_GROUP = 16
S5_GROUPS = MIX_W // S5_GROUP
S5_STATE = 64
HY_ORDER = 2
HY_BANDS = 16
HY_INNER = 2
HY_MIN_DECAY = math.log(1e-2) / 1.5
HY_MAX_DECAY = math.log(1e-2) / 0.3
GLA_HEADS = 4
GLA_DK = 64
GLA_DV = 128
GLA_RANK = 16
GLA_GATE_NORM = 16.0
HG_KEY = HG_HEADS * HG_DK
GLA_KEY = GLA_HEADS * GLA_DK

LANES = 128
MXU_DIM = 256
VMEM_LIMIT_BYTES = 56 * 1024 * 1024
FFN_TOKEN_TILE = 512
FFN_HIDDEN_CHUNK = MXU_DIM
PROJ_TOKEN_TILE = 1024
MERGE_TOKEN_TILE = 256
REC_CHUNK = 64
REC_BLOCK = 512
REC_EXP_CLAMP = 80.0
REC_UNROLL = 4
S5_CHUNK = 64
FFT_N2 = LANES
FFT_K1_TILE = 8
FFT_COL_TILE = 8192

F32 = jnp.float32
BF16 = jnp.bfloat16
HIGHEST = lax.Precision.HIGHEST


def _cparams(n_axes):
    return pltpu.CompilerParams(dimension_semantics=("arbitrary",) * n_axes,
                                vmem_limit_bytes=VMEM_LIMIT_BYTES)


def _sigmoid(x):
    return 1.0 / (1.0 + jnp.exp(-x))


def _log_sigmoid(x):
    return -(jnp.maximum(-x, 0.0) + jnp.log(1.0 + jnp.exp(-jnp.abs(x))))


def _ffn_body(x_ref, mod_ref, gpre_ref, gpost_ref, win_ref, wout_ref, o_ref, acc_ref, *, hidden, chunk):
    x = x_ref[0]
    xn = x * lax.rsqrt(jnp.mean(x * x, axis=-1, keepdims=True) + EPS) * gpre_ref[...]
    shift = mod_ref[0, 0:1, :]
    scale = mod_ref[0, 1:2, :]
    gate = mod_ref[0, 2:3, :]
    h = (xn * (1.0 + scale) + shift).astype(BF16)
    for j in range(hidden // chunk):
        a = jnp.dot(h, win_ref[:, j * chunk:(j + 1) * chunk], preferred_element_type=F32)
        b = jnp.dot(h, win_ref[:, hidden + j * chunk:hidden + (j + 1) * chunk],
                    preferred_element_type=F32)
        g = (a * _sigmoid(a) * b).astype(BF16)
        part = jnp.dot(g, wout_ref[j * chunk:(j + 1) * chunk, :], preferred_element_type=F32)
        if j == 0:
            acc_ref[...] = part
        else:
            acc_ref[...] += part
    y = acc_ref[...]
    yn = y * lax.rsqrt(jnp.mean(y * y, axis=-1, keepdims=True) + EPS) * gpost_ref[...]
    o_ref[0] = x + 0.5 * gate * yn


def ffn_sublayer(x, mods, g_pre, g_post, w_in, w_out):
    b, l, d = x.shape
    hidden = w_out.shape[0]
    tm = min(FFN_TOKEN_TILE, l)
    assert l % tm == 0 and hidden % FFN_HIDDEN_CHUNK == 0
    const = lambda i, j: (0, 0)
    return pl.pallas_call(
        functools.partial(_ffn_body, hidden=hidden, chunk=FFN_HIDDEN_CHUNK),
        grid=(b, l // tm),
        in_specs=[
            pl.BlockSpec((1, tm, d), lambda i, j: (i, j, 0)),
            pl.BlockSpec((1, 3, d), lambda i, j: (i, 0, 0)),
            pl.BlockSpec((1, d), const),
            pl.BlockSpec((1, d), const),
            pl.BlockSpec((d, 2 * hidden), const, pipeline_mode=pl.Buffered(1)),
            pl.BlockSpec((hidden, d), const, pipeline_mode=pl.Buffered(1)),
        ],
        out_specs=pl.BlockSpec((1, tm, d), lambda i, j: (i, j, 0)),
        out_shape=jax.ShapeDtypeStruct((b, l, d), F32),
        scratch_shapes=[pltpu.VMEM((tm, d), F32)],
        compiler_params=_cparams(2),
        name="ffn_sublayer",
    )(x, mods, g_pre.reshape(1, d), g_post.reshape(1, d), w_in, w_out)


def _mm_body(*refs, n_extra, epilogue):
    a_ref, b_ref = refs[0], refs[1]
    extras = refs[2:2 + n_extra]
    o_ref = refs[2 + n_extra]
    acc = jnp.dot(a_ref[0], b_ref[0], preferred_element_type=F32)
    if epilogue is not None:
        acc = epilogue(acc, *extras)
    o_ref[0] = acc.astype(o_ref.dtype)


def matmul(a, b, *, out_dtype, tm, tn, epilogue=None, out_cols=None, extras=(), extra_specs=(), name):
    ga, m, k = a.shape
    gb, k2, n = b.shape
    assert k == k2 and m % tm == 0 and n % tn == 0
    g = max(ga, gb)
    oc = tn if out_cols is None else out_cols
    assert out_cols is None or tn == n
    a_map = (lambda gi, i, j: (gi, i, 0)) if ga > 1 else (lambda gi, i, j: (0, i, 0))
    b_map = (lambda gi, i, j: (gi, 0, j)) if gb > 1 else (lambda gi, i, j: (0, 0, j))
    return pl.pallas_call(
        functools.partial(_mm_body, n_extra=len(extras), epilogue=epilogue),
        grid=(g, m // tm, n // tn),
        in_specs=[pl.BlockSpec((1, tm, k), a_map), pl.BlockSpec((1, k, tn), b_map), *extra_specs],
        out_specs=pl.BlockSpec((1, tm, oc), lambda gi, i, j: (gi, i, j)),
        out_shape=jax.ShapeDtypeStruct((g, m, (n // tn) * oc), out_dtype),
        compiler_params=_cparams(3),
        name=name,
    )(a, b, *extras)


def _rec_core(d, load_qkl, v_ref, s0_ref, o_ref, sfin_ref, st_ref, *, heads, dk, dv, nchunks):
    c_len = REC_CHUNK
    j = pl.program_id(2)

    @pl.when(j == 0)
    def _():
        st_ref[...] = s0_ref[0, 0]

    row = lax.broadcasted_iota(jnp.int32, (c_len, c_len), 0)
    col = lax.broadcasted_iota(jnp.int32, (c_len, c_len), 1)
    sign = 1 - 2 * d
    tri = (row - col) * sign >= 0
    trif = tri.astype(F32)

    def chunk(ci, carry):
        c = ci + d * (nchunks - 1 - 2 * ci)
        r0 = pl.multiple_of(c * c_len, c_len)
        q, k, lf = load_qkl(r0)
        v = v_ref[0, pl.ds(r0, c_len), :].astype(F32)
        g = jnp.dot(trif, lf, precision=HIGHEST, preferred_element_type=F32)
        gtot = jnp.sum(lf, axis=0, keepdims=True)
        gmid = g[c_len // 2:c_len // 2 + 1, :]
        qs = (q * jnp.exp(g)).astype(BF16)
        ks = (k * jnp.exp(gtot - g)).astype(BF16)
        qh = (q * jnp.exp(jnp.minimum(g - gmid, REC_EXP_CLAMP))).astype(BF16)
        kh = (k * jnp.exp(jnp.minimum(gmid - g, REC_EXP_CLAMP))).astype(BF16)
        dec = jnp.exp(gtot)
        outs = []
        for h in range(heads):
            ksl = slice(h * dk, (h + 1) * dk)
            vh = v[:, h * dv:(h + 1) * dv]
            vhb = vh.astype(BF16)
            sc = lax.dot_general(qh[:, ksl], kh[:, ksl], (((1,), (1,)), ((), ())),
                                 preferred_element_type=F32)
            sc = jnp.where(tri, sc, 0.0).astype(BF16)
            st = st_ref[h]
            o = jnp.dot(sc, vhb, preferred_element_type=F32)
            o = o + lax.dot_general(qs[:, ksl], st.astype(BF16), (((1,), (1,)), ((), ())),
                                    preferred_element_type=F32)
            upd = jnp.dot(vh.T.astype(BF16), ks[:, ksl], preferred_element_type=F32)
            st_ref[h] = st * dec[:, ksl] + upd
            outs.append(o)
        o_ref[0, 0, pl.ds(r0, c_len), :] = jnp.concatenate(outs, axis=1)
        return carry

    lax.fori_loop(0, nchunks, chunk, 0, unroll=REC_UNROLL)

    @pl.when(j == pl.num_programs(2) - 1)
    def _():
        sfin_ref[0, 0] = st_ref[...]


def _hg_body(hq_ref, hf_ref, v_ref, gp_ref, s0_ref, o_ref, sfin_ref, st_ref, *, nchunks):
    d = pl.program_id(1)
    log_lb = gp_ref[0, 0:1, :]
    log_1mlb = gp_ref[0, 1:2, :]
    one_mlb = gp_ref[0, 2:3, :]

    def load_qkl(r0):
        hq = hq_ref[0, pl.ds(r0, REC_CHUNK), :]
        z = hf_ref[0, pl.ds(r0, REC_CHUNK), :]
        q = hq * _sigmoid(hq) * (HG_DK ** -0.5)
        k = one_mlb * _sigmoid(-z)
        b = log_1mlb + _log_sigmoid(z)
        lf = jnp.maximum(log_lb, b) + jnp.log(1.0 + jnp.exp(-jnp.abs(log_lb - b)))
        return q, k, lf

    _rec_core(d, load_qkl, v_ref, s0_ref, o_ref, sfin_ref, st_ref,
              heads=HG_HEADS, dk=HG_DK, dv=HG_DV, nchunks=nchunks)


def _gla_body(q_ref, k_ref, v_ref, z_ref, gb_ref, s0_ref, o_ref, sfin_ref, st_ref, *, nchunks):
    d = pl.program_id(1)

    def load_qkl(r0):
        q = q_ref[0, pl.ds(r0, REC_CHUNK), :] * (GLA_DK ** -0.5)
        k = k_ref[0, pl.ds(r0, REC_CHUNK), :]
        z = z_ref[0, pl.ds(r0, REC_CHUNK), :] + gb_ref[0]
        return q, k, _log_sigmoid(z) * (1.0 / GLA_GATE_NORM)

    _rec_core(d, load_qkl, v_ref, s0_ref, o_ref, sfin_ref, st_ref,
              heads=GLA_HEADS, dk=GLA_DK, dv=GLA_DV, nchunks=nchunks)


def _rec_geometry(l):
    t = min(REC_BLOCK, l)
    assert l % t == 0 and t % REC_CHUNK == 0
    return t, l // t


def _rec_row_block(nb):
    return lambda d, j: j + d * (nb - 1 - 2 * j)


def hgrn2_recurrence(proj, gate_params, s0):
    b, l, _ = proj.shape
    t, nb = _rec_geometry(l)
    rb = _rec_row_block(nb)
    w = HG_KEY
    st_shape = (HG_HEADS, HG_DV, HG_DK)
    return pl.pallas_call(
        functools.partial(_hg_body, nchunks=t // REC_CHUNK),
        grid=(b, 2, nb),
        in_specs=[
            pl.BlockSpec((1, t, w), lambda i, d, j: (i, rb(d, j), 0)),
            pl.BlockSpec((1, t, w), lambda i, d, j: (i, rb(d, j), 1 + d)),
            pl.BlockSpec((1, t, w), lambda i, d, j: (i, rb(d, j), 3)),
            pl.BlockSpec((1, 3, w), lambda i, d, j: (d, 0, 0)),
            pl.BlockSpec((1, 1) + st_shape, lambda i, d, j: (i, d, 0, 0, 0)),
        ],
        out_specs=[
            pl.BlockSpec((1, 1, t, w), lambda i, d, j: (d, i, rb(d, j), 0)),
            pl.BlockSpec((1, 1) + st_shape, lambda i, d, j: (i, d, 0, 0, 0)),
        ],
        out_shape=[jax.ShapeDtypeStruct((2, b, l, w), F32),
                   jax.ShapeDtypeStruct((b, 2) + st_shape, F32)],
        scratch_shapes=[pltpu.VMEM(st_shape, F32)],
        compiler_params=_cparams(3),
        name="hgrn2_recurrence",
    )(proj, proj, proj, gate_params, s0)


def gla_recurrence(proj, gate_b, s0):
    b, l, _ = proj.shape
    t, nb = _rec_geometry(l)
    rb = _rec_row_block(nb)
    st_shape = (GLA_HEADS, GLA_DV, GLA_DK)
    z_block = (2 * GLA_KEY + 2 * MIX_W) // GLA_KEY
    return pl.pallas_call(
        functools.partial(_gla_body, nchunks=t // REC_CHUNK),
        grid=(b, 2, nb),
        in_specs=[
            pl.BlockSpec((1, t, GLA_KEY), lambda i, d, j: (i, rb(d, j), 0)),
            pl.BlockSpec((1, t, GLA_KEY), lambda i, d, j: (i, rb(d, j), 1)),
            pl.BlockSpec((1, t, MIX_W), lambda i, d, j: (i, rb(d, j), 1)),
            pl.BlockSpec((1, t, GLA_KEY), lambda i, d, j: (i, rb(d, j), z_block + d)),
            pl.BlockSpec((1, 1, GLA_KEY), lambda i, d, j: (d, 0, 0)),
            pl.BlockSpec((1, 1) + st_shape, lambda i, d, j: (i, d, 0, 0, 0)),
        ],
        out_specs=[
            pl.BlockSpec((1, 1, t, MIX_W), lambda i, d, j: (d, i, rb(d, j), 0)),
            pl.BlockSpec((1, 1) + st_shape, lambda i, d, j: (i, d, 0, 0, 0)),
        ],
        out_shape=[jax.ShapeDtypeStruct((2, b, l, MIX_W), F32),
                   jax.ShapeDtypeStruct((b, 2) + st_shape, F32)],
        scratch_shapes=[pltpu.VMEM(st_shape, F32)],
        compiler_params=_cparams(3),
        name="gla_recurrence",
    )(proj, proj, proj, proj, gate_b, s0)


def _s5_scan_body(e_ref, la_ref, lb_ref, x0_ref, xs_ref, xfin_ref, *, nchunks):
    d = pl.program_id(0)
    la = la_ref[0]
    lb = lb_ref[0]

    def step(kk, x):
        c = kk + d * (nchunks - 1 - 2 * kk)
        xs_ref[0, c] = x
        return la * x + lb * pltpu.roll(x, LANES // 2, axis=1) + e_ref[0, c]

    xfin_ref[0] = lax.fori_loop(0, nchunks, step, x0_ref[0])


def s5_chunk_scan(e, lam_a, lam_b, x0):
    _, nc, rows, _ = e.shape
    g = lam_a.shape[1]
    return pl.pallas_call(
        functools.partial(_s5_scan_body, nchunks=nc),
        grid=(2, rows // g),
        in_specs=[
            pl.BlockSpec((1, nc, g, LANES), lambda d, i: (d, 0, i, 0)),
            pl.BlockSpec((1, g, LANES), lambda d, i: (d, 0, 0)),
            pl.BlockSpec((1, g, LANES), lambda d, i: (d, 0, 0)),
            pl.BlockSpec((1, g, LANES), lambda d, i: (d, i, 0)),
        ],
        out_specs=[
            pl.BlockSpec((1, nc, g, LANES), lambda d, i: (d, 0, i, 0)),
            pl.BlockSpec((1, g, LANES), lambda d, i: (d, i, 0)),
        ],
        out_shape=[jax.ShapeDtypeStruct(e.shape, F32), jax.ShapeDtypeStruct(x0.shape, F32)],
        compiler_params=_cparams(2),
        name="s5_chunk_scan",
    )(e, lam_a, lam_b, x0)


def s5_matrices(a_re, a_im, log_dt, b_re, b_im, c_re, c_im):
    c_len = S5_CHUNK
    hp = dict(precision=HIGHEST)
    lam = lax.complex(a_re.astype(F32), a_im.astype(F32))
    dt = jnp.exp(log_dt.astype(F32))[:, :, None]
    lam_dt = lam * dt
    lam_bar = jnp.exp(lam_dt)
    bmat = lax.complex(b_re.astype(F32), b_im.astype(F32))
    cmat = lax.complex(c_re.astype(F32), c_im.astype(F32))
    b_bar = ((lam_bar - 1.0) / lam)[:, :, :, None] * bmat[None]
    tau = jnp.arange(c_len + 1, dtype=F32)[None, :, None, None]
    pw = jnp.exp(lam_dt[:, None] * tau)
    kern = jnp.einsum('ghp,dtgp,dgpk->dtghk', cmat, pw[:, :c_len], b_bar, **hp).real
    k_all = jnp.concatenate([jnp.flip(kern[1, 1:], axis=0), (kern[0, 0] + kern[1, 0])[None], kern[0, 1:]],
                            axis=0)
    i_idx = jnp.arange(c_len)
    lag = i_idx[None, :] - i_idx[:, None] + (c_len - 1)
    m5 = k_all[lag]
    m_mat = m5.transpose(2, 0, 4, 1, 3).reshape(S5_GROUPS, c_len * S5_GROUP, c_len * S5_GROUP)
    qf = pw[0, c_len - 1 - i_idx][:, :, :, None] * b_bar[0][None]
    qb = pw[1, i_idx][:, :, :, None] * b_bar[1][None]
    q4 = jnp.stack([qf.real, qf.imag, qb.real, qb.imag], axis=0)
    q_mat = q4.transpose(2, 1, 4, 0, 3).reshape(S5_GROUPS, c_len * S5_GROUP, 4 * S5_STATE)
    wf = cmat[None] * pw[0, 1 + i_idx][:, :, None, :]
    wb = cmat[None] * pw[1, c_len - i_idx][:, :, None, :]
    p4 = jnp.stack([wf.real, -wf.imag, wb.real, -wb.imag], axis=0)
    p_mat = p4.transpose(2, 0, 4, 1, 3).reshape(S5_GROUPS, 4 * S5_STATE, c_len * S5_GROUP)
    mp_mat = jnp.concatenate([m_mat, p_mat], axis=1)
    lc = pw[:, c_len]
    lam_a = jnp.concatenate([lc.real, lc.real], axis=-1)
    lam_b = jnp.concatenate([-lc.imag, lc.imag], axis=-1)
    return q_mat.astype(BF16), mp_mat.astype(BF16), lam_a, lam_b


def s5_mixer(u, mats, x0, with_output):
    q_mat, mp_mat, lam_a, lam_b = mats
    b, l, _ = u.shape
    c_len = S5_CHUNK
    nc = l // c_len
    rows = b * nc
    ug = u.reshape(b, nc, c_len, S5_GROUPS, S5_GROUP).transpose(3, 0, 1, 2, 4).reshape(
        S5_GROUPS, rows, c_len * S5_GROUP)
    tm = min(rows, 512)
    e = matmul(ug, q_mat, out_dtype=F32, tm=tm, tn=4 * S5_STATE, name="s5_local_states")
    e = e.reshape(S5_GROUPS, b, nc, 2, LANES).transpose(3, 2, 1, 0, 4).reshape(2, nc, b * S5_GROUPS, LANES)
    xs, xfin = s5_chunk_scan(e, lam_a, lam_b, x0)
    if not with_output:
        return None, xfin
    xg = xs.reshape(2, nc, b, S5_GROUPS, LANES).transpose(3, 2, 1, 0, 4).reshape(
        S5_GROUPS, rows, 2 * LANES).astype(BF16)
    y = matmul(jnp.concatenate([ug, xg], axis=-1), mp_mat, out_dtype=F32, tm=tm,
               tn=c_len * S5_GROUP, name="s5_outputs")
    y = y.reshape(S5_GROUPS, b, nc, c_len, S5_GROUP).transpose(1, 2, 3, 0, 4).reshape(b, l, MIX_W)
    return y, xfin


def _half_spectrum_rows(n1):
    tk = FFT_K1_TILE
    return -(-(n1 // 2 + 1) // tk) * tk


def _dft_tables(l):
    n = 2 * l
    n1 = n // FFT_N2
    n1p = _half_spectrum_rows(n1)
    k1 = np.arange(n1p)[:, None]
    m1 = np.arange(n1 // 2)[None, :]
    th = 2.0 * np.pi * k1 * m1 / n1
    f_fwd = np.concatenate([np.cos(th), -np.sin(th)], axis=0)
    wgt = np.where((k1 == 0) | (k1 == n1 // 2), 1.0, 2.0) * (k1 <= n1 // 2) / n
    f_inv = np.concatenate([(np.cos(th) * wgt).T, (-np.sin(th) * wgt).T], axis=1)
    ph = 2.0 * np.pi * np.arange(n1p)[:, None] * np.arange(FFT_N2)[None, :] / n
    tw = np.stack([np.cos(ph), np.sin(ph)], axis=0)[..., None]
    a2 = 2.0 * np.pi * np.arange(FFT_N2)[:, None] * np.arange(FFT_N2)[None, :] / FFT_N2
    wr, wi = np.cos(a2), -np.sin(a2)
    w_fwd = np.block([[wr, -wi], [wi, wr]])
    w_inv = np.block([[wr, wi], [-wi, wr]])
    return (jnp.asarray(f_fwd[None], BF16), jnp.asarray(f_inv[None], BF16), jnp.asarray(tw, F32),
            jnp.asarray(w_fwd, BF16), jnp.asarray(w_inv, BF16))


def _inner_dft(a_ref, tw_ref, wf_ref, kk):
    ar = a_ref[0, 0, kk].astype(F32)
    ai = a_ref[0, 1, kk].astype(F32)
    c = tw_ref[0, kk]
    s = tw_ref[1, kk]
    stack = jnp.concatenate([ar * c + ai * s, ai * c - ar * s], axis=0).astype(BF16)
    return jnp.dot(wf_ref[...], stack, preferred_element_type=F32), c, s


def _spectral_filter_body(a_ref, tw_ref, wf_ref, wi_ref, kf_ref, o_ref):
    for kk in range(FFT_K1_TILE):
        x, c, s = _inner_dft(a_ref, tw_ref, wf_ref, kk)
        kf = kf_ref[0, kk]
        y = jnp.concatenate([x[:FFT_N2] * kf, x[FFT_N2:] * kf], axis=0).astype(BF16)
        bm = jnp.dot(wi_ref[...], y, preferred_element_type=F32)
        br, bi = bm[:FFT_N2], bm[FFT_N2:]
        o_ref[0, 0, kk] = (br * c - bi * s).astype(o_ref.dtype)
        o_ref[0, 1, kk] = (bi * c + br * s).astype(o_ref.dtype)


def _filter_spectrum_body(a_ref, tw_ref, wf_ref, h0_ref, o_ref):
    for kk in range(FFT_K1_TILE):
        x, _, _ = _inner_dft(a_ref, tw_ref, wf_ref, kk)
        o_ref[0, kk] = 2.0 * x[:FFT_N2] - h0_ref[0]


def _outer_dft(z, f_fwd):
    g, l, c = z.shape
    n1 = 2 * l // FFT_N2
    n1p = f_fwd.shape[1] // 2
    cols = FFT_N2 * c
    zc = z.astype(BF16).reshape(g, n1 // 2, cols)
    a = matmul(f_fwd, zc, out_dtype=BF16, tm=2 * n1p, tn=min(FFT_COL_TILE, cols), name="hyena_outer_dft")
    return a.reshape(g, 2, n1p, FFT_N2, c)


def _spectral_specs(c):
    tk = FFT_K1_TILE
    return (pl.BlockSpec((1, 2, tk, FFT_N2, c), lambda i, j: (i, 0, j, 0, 0)),
            pl.BlockSpec((2, tk, FFT_N2, 1), lambda i, j: (0, j, 0, 0)),
            pl.BlockSpec((2 * FFT_N2, 2 * FFT_N2), lambda i, j: (0, 0)))


def filter_spectrum(h, tables):
    f_fwd, _, tw, w_fwd, _ = tables
    g, l, c = h.shape
    n1p = f_fwd.shape[1] // 2
    a = _outer_dft(h, f_fwd)
    tk = FFT_K1_TILE
    return pl.pallas_call(
        _filter_spectrum_body,
        grid=(g, n1p // tk),
        in_specs=[*_spectral_specs(c), pl.BlockSpec((1, 1, c), lambda i, j: (i, 0, 0))],
        out_specs=pl.BlockSpec((1, tk, FFT_N2, c), lambda i, j: (i, j, 0, 0)),
        out_shape=jax.ShapeDtypeStruct((g, n1p, FFT_N2, c), F32),
        compiler_params=_cparams(2),
        name="hyena_filter_spectrum",
    )(a, tw, w_fwd, h[:, 0:1, :])


def long_conv_gated(z, kf, kf_sel, bias, gate, tables):
    f_fwd, f_inv, tw, w_fwd, w_inv = tables
    g, l, c = z.shape
    n1 = 2 * l // FFT_N2
    n1p = f_fwd.shape[1] // 2
    cols = FFT_N2 * c
    a = _outer_dft(z, f_fwd)
    tk = FFT_K1_TILE
    a_spec, tw_spec, w_spec = _spectral_specs(c)
    bm = pl.pallas_call(
        _spectral_filter_body,
        grid=(g, n1p // tk),
        in_specs=[a_spec, tw_spec, w_spec, w_spec,
                  pl.BlockSpec((1, tk, FFT_N2, c), lambda i, j: (kf_sel(i), j, 0, 0))],
        out_specs=a_spec,
        out_shape=jax.ShapeDtypeStruct(a.shape, BF16),
        compiler_params=_cparams(2),
        name="hyena_spectral_filter",
    )(a, tw, w_fwd, w_inv, kf)
    tn = min(FFT_COL_TILE, cols)
    bias_t = jnp.tile(bias, (1, FFT_N2)).reshape(bias.shape[0], 1, cols)

    def epilogue(acc, z_ref, gate_ref, bias_ref):
        return gate_ref[0].astype(F32) * (acc + bias_ref[0] * z_ref[0].astype(F32))

    row_spec = pl.BlockSpec((1, n1 // 2, tn), lambda gi, i, j: (gi, 0, j))
    y = matmul(f_inv, bm.reshape(g, 2 * n1p, cols), out_dtype=BF16, tm=n1 // 2, tn=tn, epilogue=epilogue,
               extras=(z.reshape(g, n1 // 2, cols), gate.reshape(g, n1 // 2, cols), bias_t),
               extra_specs=(row_spec, row_spec, pl.BlockSpec((1, 1, tn), lambda gi, i, j: (kf_sel(gi), 0, j))),
               name="hyena_outer_idft")
    return y.reshape(g, l, c)


def hyena_filters(l, fw_in, fb_in, fw_mid, fb_mid, fw_out, freq):
    hp = dict(precision=HIGHEST)
    t = jnp.linspace(0.0, 1.0, l, dtype=F32)[:, None]
    pos = jnp.arange(l, dtype=F32)[:, None]
    bands = jnp.linspace(1e-4, HY_BANDS - 1, HY_BANDS, dtype=F32)
    ang = (2.0 * math.pi / l) * pos * bands
    feat = jnp.concatenate([t, jnp.cos(ang), -jnp.sin(ang)], axis=-1)
    w = freq.astype(F32)
    h = jnp.sin(w * (jnp.dot(feat, fw_in.astype(F32), **hp) + fb_in.astype(F32)))
    for j in range(HY_INNER):
        h = jnp.sin(w * (jnp.dot(h, fw_mid[j].astype(F32), **hp) + fb_mid[j].astype(F32)))
    h = jnp.dot(h, fw_out.astype(F32), **hp)
    deltas = jnp.abs(jnp.linspace(HY_MIN_DECAY, HY_MAX_DECAY, HY_ORDER * MIX_W, dtype=F32))
    h = h * jnp.exp(-t * deltas)
    return h * lax.rsqrt(2.0 * jnp.sum(h * h, axis=0, keepdims=True) - h[:1] ** 2 + EPS)


def depthwise_conv3(u, w, bias, width):
    g, l, c = u.shape
    ur = u.reshape(g, l // width, width, c)
    zero = jnp.zeros_like(ur[:, :, :1])
    prev = jnp.concatenate([zero, ur[:, :, :-1]], axis=2)
    nxt = jnp.concatenate([ur[:, :, 1:], zero], axis=2)
    taps = w[0] * prev.astype(F32) + w[1] * ur.astype(F32) + w[2] * nxt.astype(F32) + bias
    return taps.astype(u.dtype).reshape(g, l, c)


def _merge_body(x_ref, h_ref, br_ref, mod_ref, gpost_ref, wg_ref, wb_ref, wo_ref, o_ref):
    d = x_ref.shape[-1]
    h = h_ref[0]
    merged = None
    for n in range(N_BRANCH):
        gate_n = _sigmoid(jnp.dot(h, wg_ref[:, n * d:(n + 1) * d], preferred_element_type=F32))
        t_n = jnp.dot(br_ref[n, 0], wb_ref[n], preferred_element_type=F32)
        merged = gate_n * t_n if merged is None else merged + gate_n * t_n
    y = jnp.dot(merged.astype(BF16), wo_ref[...], preferred_element_type=F32)
    yn = y * lax.rsqrt(jnp.mean(y * y, axis=-1, keepdims=True) + EPS) * gpost_ref[...]
    o_ref[0] = x_ref[0] + mod_ref[0] * yn


def merge_branches(x, h, branches, res_gate, g_post, w_gates, w_branch, w_out):
    b, l, d = x.shape
    tm = min(MERGE_TOKEN_TILE, l)
    const2 = lambda i, j: (0, 0)
    return pl.pallas_call(
        _merge_body,
        grid=(b, l // tm),
        in_specs=[
            pl.BlockSpec((1, tm, d), lambda i, j: (i, j, 0)),
            pl.BlockSpec((1, tm, d), lambda i, j: (i, j, 0)),
            pl.BlockSpec((N_BRANCH, 1, tm, MIX_W), lambda i, j: (0, i, j, 0)),
            pl.BlockSpec((1, 1, d), lambda i, j: (i, 0, 0)),
            pl.BlockSpec((1, d), const2),
            pl.BlockSpec((d, N_BRANCH * d), const2, pipeline_mode=pl.Buffered(1)),
            pl.BlockSpec((N_BRANCH, MIX_W, d), lambda i, j: (0, 0, 0), pipeline_mode=pl.Buffered(1)),
            pl.BlockSpec((d, d), const2, pipeline_mode=pl.Buffered(1)),
        ],
        out_specs=pl.BlockSpec((1, tm, d), lambda i, j: (i, j, 0)),
        out_shape=jax.ShapeDtypeStruct((b, l, d), F32),
        compiler_params=_cparams(2),
        name="merge_branches",
    )(x, h, branches, res_gate, g_post.reshape(1, d), w_gates, w_branch, w_out)


def _rmsnorm(x, g):
    return x * lax.rsqrt(jnp.mean(x * x, axis=-1, keepdims=True) + EPS) * g


def _project(h, w, out_dtype, name):
    b, l, d = h.shape
    n = w.shape[1]
    tm = min(PROJ_TOKEN_TILE, b * l)
    out = matmul(h.reshape(1, b * l, d), w[None], out_dtype=out_dtype, tm=tm, tn=n, name=name)
    return out.reshape(b, l, n)


def _gated_head_norm(o, gnorm, g, nheads, dv):
    _, b, l, _ = o.shape
    s = (o[0] + o[1]).reshape(b, l, nheads, dv)
    y = _rmsnorm(s, gnorm).reshape(b, l, nheads * dv)
    return (y * (g * _sigmoid(g))).astype(BF16)


def _lower_bound(logits, layer):
    cum = jnp.cumsum(jax.nn.softmax(logits.astype(F32), axis=0), axis=0)
    return cum[layer] - cum[0]


def _layer_params(l, w_in, hg_lb_fwd, hg_lb_bwd, s5_a_re, s5_a_im, s5_log_dt, s5_b_re, s5_b_im, s5_c_re,
                  s5_c_im, gla_gate_w, gla_gate_b):
    d = w_in.shape[1]
    cs = np.cumsum((0, HG_KEY, HG_KEY, HG_KEY, MIX_W, MIX_W, MIX_W, (HY_ORDER + 1) * MIX_W,
                    GLA_KEY, GLA_KEY, MIX_W, MIX_W, 2 * GLA_RANK, N_BRANCH * D_MODEL))
    w = w_in[l].astype(BF16)
    w_lr = w_in[l][:, cs[11]:cs[12]].astype(F32)
    w_gate = [jnp.dot(w_lr[:, dd * GLA_RANK:(dd + 1) * GLA_RANK], gla_gate_w[l, dd].astype(F32),
                      precision=HIGHEST).astype(BF16) for dd in range(2)]
    p = dict(
        w_hg=w[:, cs[0]:cs[5]], w_s5=w[:, cs[5]:cs[6]], w_hy=w[:, cs[6]:cs[7]],
        w_gla=jnp.concatenate([w[:, cs[7]:cs[11]], *w_gate], axis=1),
        w_gates=w[:, cs[12]:cs[13]])
    lb = jnp.stack([_lower_bound(hg_lb_fwd, l), _lower_bound(hg_lb_bwd, l)])
    p['hg_gate'] = jnp.stack([jnp.log(lb), jnp.log1p(-lb), 1.0 - lb], axis=1)
    p['gla_gb'] = gla_gate_b[l].reshape(2, 1, GLA_KEY).astype(F32)
    p['s5'] = s5_matrices(s5_a_re[l], s5_a_im[l], s5_log_dt[l], s5_b_re[l], s5_b_im[l], s5_c_re[l], s5_c_im[l])
    return p


def _recurrent_branches(h, p, states):
    s_hg, s_gla, x_s5 = states
    pr_hg = _project(h, p['w_hg'], F32, "proj_hgrn2")
    pr_gla = _project(h, p['w_gla'], F32, "proj_gla")
    u_s5 = _project(h, p['w_s5'], BF16, "proj_s5")
    o_hg, s_hg = hgrn2_recurrence(pr_hg, p['hg_gate'], s_hg)
    o_gla, s_gla = gla_recurrence(pr_gla, p['gla_gb'], s_gla)
    return pr_hg, pr_gla, u_s5, o_hg, o_gla, (s_hg, s_gla, x_s5)


def kernel(x, c, ctx, c_ctx, ada_w, ada_b, norm_pre, norm_post, ffn1_w_in, ffn1_w_out,
           ffn2_w_in, ffn2_w_out, w_in, hg_lb_fwd, hg_lb_bwd, hg_gnorm, s5_a_re, s5_a_im,
           s5_log_dt, s5_b_re, s5_b_im, s5_c_re, s5_c_im, s5_d, s5_glu_w, s5_glu_b,
           hy_conv_w, hy_conv_b, hy_fw_in, hy_fb_in, hy_fw_mid, hy_fb_mid, hy_fw_out,
           hy_freq, hy_bias, gla_gate_w, gla_gate_b, gla_gnorm, w_branch, w_out):
    b, seq, d = x.shape
    lc = ctx.shape[1]
    depth = ada_w.shape[0]
    hp = dict(precision=HIGHEST)
    c_act = c * _sigmoid(c)
    cc_act = c_ctx * _sigmoid(c_ctx)
    tables = _dft_tables(seq)
    xc = ctx
    for l in range(depth):
        last = l == depth - 1
        p = _layer_params(l, w_in, hg_lb_fwd, hg_lb_bwd, s5_a_re, s5_a_im, s5_log_dt, s5_b_re, s5_b_im,
                          s5_c_re, s5_c_im, gla_gate_w, gla_gate_b)
        mod = (jnp.dot(c_act, ada_w[l], **hp) + ada_b[l]).reshape(b, N_MOD, d)
        modc = jnp.broadcast_to((jnp.dot(cc_act, ada_w[l], **hp) + ada_b[l]).reshape(1, N_MOD, d), (b, N_MOD, d))
        w1i, w1o = ffn1_w_in[l].astype(BF16), ffn1_w_out[l].astype(BF16)
        w2i, w2o = ffn2_w_in[l].astype(BF16), ffn2_w_out[l].astype(BF16)
        x = ffn_sublayer(x, mod[:, 0:3], norm_pre[l, 0], norm_post[l, 0], w1i, w1o)
        xc = ffn_sublayer(xc, modc[:, 0:3], norm_pre[l, 0], norm_post[l, 0], w1i, w1o)
        hc = (_rmsnorm(xc, norm_pre[l, 1]) * (1.0 + modc[:, 4:5]) + modc[:, 3:4]).astype(BF16)
        hx = (_rmsnorm(x, norm_pre[l, 1]) * (1.0 + mod[:, 4:5]) + mod[:, 3:4]).astype(BF16)

        zero = (jnp.zeros((b, 2, HG_HEADS, HG_DV, HG_DK), F32), jnp.zeros((b, 2, GLA_HEADS, GLA_DV, GLA_DK), F32),
                jnp.zeros((2, b * S5_GROUPS, LANES), F32))
        c_hg, c_gla, c_u, c_ohg, c_ogla, st = _recurrent_branches(hc, p, zero)
        c_y5, c_x5 = s5_mixer(c_u, p['s5'], st[2], not last)
        st = (st[0], st[1], c_x5)
        x_hg, x_gla, x_u, x_ohg, x_ogla, st2 = _recurrent_branches(hx, p, st)
        x_y5, _ = s5_mixer(x_u, p['s5'], st[2], True)

        hy_x = _project(hx, p['w_hy'], BF16, "proj_hyena")
        uc = depthwise_conv3(hy_x, hy_conv_w[l], hy_conv_b[l], GRID_W)
        filt = hyena_filters(seq, hy_fw_in[l], hy_fb_in[l], hy_fw_mid[l], hy_fb_mid[l], hy_fw_out[l], hy_freq[l])
        filt = filt.reshape(seq, HY_ORDER, MIX_W).transpose(1, 0, 2)
        if not last:
            hy_c = _project(hc, p['w_hy'], BF16, "proj_hyena_ctx")
            ucc = depthwise_conv3(hy_c, hy_conv_w[l], hy_conv_b[l], lc)
            uc = jnp.concatenate([uc, jnp.pad(ucc, ((0, 0), (0, seq - lc), (0, 0)))], axis=0)
            filt_c = hyena_filters(lc, hy_fw_in[l], hy_fb_in[l], hy_fw_mid[l], hy_fb_mid[l], hy_fw_out[l],
                                   hy_freq[l]).reshape(lc, HY_ORDER, MIX_W).transpose(1, 0, 2)
            filt = jnp.concatenate([filt, jnp.pad(filt_c, ((0, 0), (0, seq - lc), (0, 0)))], axis=0)
        kf = filter_spectrum(filt, tables)
        z = uc[:, :, :MIX_W]
        for n in range(HY_ORDER):
            gate = uc[:, :, (n + 1) * MIX_W:(n + 2) * MIX_W]
            kf_n = kf[n::HY_ORDER]
            bias_n = jnp.broadcast_to(hy_bias[l, n].astype(F32), (kf_n.shape[0], MIX_W))
            z = long_conv_gated(z, kf_n, lambda i: i // b, bias_n, gate, tables)

        def finish(xres, hmix, pr_hg, pr_gla, u5, o_hg, o_gla, y5, zhy, m):
            br_hg = _gated_head_norm(o_hg, hg_gnorm[l], pr_hg[:, :, 4 * MIX_W:5 * MIX_W], HG_HEADS, HG_DV)
            br_gla = _gated_head_norm(o_gla, gla_gnorm[l], pr_gla[:, :, 2 * GLA_KEY + MIX_W:2 * GLA_KEY + 2 * MIX_W],
                                      GLA_HEADS, GLA_DV)
            y = jax.nn.gelu(y5 + s5_d[l].astype(F32) * u5.astype(F32)).astype(BF16)
            bb, ll, _ = y.shape

            def glu(acc, bias_ref):
                t = acc + bias_ref[...]
                return t[:, :MIX_W] * _sigmoid(t[:, MIX_W:])

            br_s5 = matmul(y.reshape(1, bb * ll, MIX_W), s5_glu_w[l].astype(BF16)[None], out_dtype=BF16,
                           tm=min(PROJ_TOKEN_TILE, bb * ll), tn=2 * MIX_W, epilogue=glu, out_cols=MIX_W,
                           extras=(s5_glu_b[l].reshape(1, 2 * MIX_W).astype(F32),),
                           extra_specs=(pl.BlockSpec((1, 2 * MIX_W), lambda gi, i, j: (0, 0)),),
                           name="s5_glu").reshape(bb, ll, MIX_W)
            branches = jnp.stack([br_hg, br_s5, zhy, br_gla])
            return merge_branches(xres, hmix, branches, m[:, 5:6], norm_post[l, 1], p['w_gates'],
                                  w_branch[l].astype(BF16), w_out[l].astype(BF16))

        x = finish(x, hx, x_hg, x_gla, x_u, x_ohg, x_ogla, x_y5, z[:b], mod)
        x = ffn_sublayer(x, mod[:, 6:9], norm_pre[l, 2], norm_post[l, 2], w2i, w2o)
        if not last:
            xc = finish(xc, hc, c_hg, c_gla, c_u, c_ohg, c_ogla, c_y5, z[b:, :lc], modc)
            xc = ffn_sublayer(xc, modc[:, 6:9], norm_pre[l, 2], norm_post[l, 2], w2i, w2o)
    return x
```
